```python
import math
import jax
import jax.numpy as jnp
from jax import lax
import numpy as np

D_MODEL = 4096
BATCH = 1
SEQ = 16384
DEPTH = 4
DEC_BATCH = 8
DEC_SEQ = 64
PAST_LEN = 2048

CHUNK = 64
N_META = 16
QBLOCK = 128
EPS = 1e-6
N_MIXERS = 3
N_SSD = (DEPTH + 2) // 3
N_S5 = (DEPTH + 1) // 3
N_MLA = DEPTH // 3

SSD_INNER = 2 * D_MODEL
SSD_HEADDIM = 64
SSD_HEADS = SSD_INNER // SSD_HEADDIM
SSD_GROUPS = 8
SSD_STATE = 128
SSD_CONV = 4
SSD_CONV_DIM = SSD_INNER + 2 * SSD_GROUPS * SSD_STATE
SSD_IN_COLS = SSD_INNER + SSD_CONV_DIM + SSD_HEADS

S5_WIDTH = 2 * D_MODEL
S5_GROUP = 16
S5_GROUPS = S5_WIDTH // S5_GROUP
S5_STATE = 64

MLA_HEADS = 64
MLA_NOPE = 128
MLA_ROPE = 64
MLA_V = 128
MLA_Q_RANK = 1024
MLA_KV_RANK = 512
MLA_WIDTH = MLA_HEADS * MLA_V
MLA_IN_COLS = MLA_Q_RANK + MLA_KV_RANK + MLA_ROPE + MLA_WIDTH
ROPE_THETA = 10000.0

kernel_name = 'hybrid_ssd_s5_mla_streaming_step'


def rms_norm(x, w):
    xf = x.astype(jnp.float32)
    y = xf * lax.rsqrt(jnp.mean(xf * xf, axis=-1, keepdims=True) + EPS)
    return (y * w.astype(jnp.float32)).astype(x.dtype)


def causal_conv(u, buf, w, b):
    full = jnp.concatenate([buf.astype(u.dtype), u], axis=1)
    out = lax.conv_general_dilated(full, w[:, None, :].astype(u.dtype), (1,), 'VALID',
                                   dimension_numbers=('NWC', 'WIO', 'NWC'),
                                   feature_group_count=u.shape[-1])
    return out + b.astype(u.dtype), full[:, -(SSD_CONV - 1):]


def ssd_scan(x, dt, A, Bm, Cm, h0):
    b, L, H, P = x.shape
    G, N = Bm.shape[2], Bm.shape[3]
    hg = H // G
    nc = L // CHUNK

    def chunks(t):
        return jnp.moveaxis(t.reshape((b, nc, CHUNK) + t.shape[2:]), 1, 0)

    xs = chunks(x.reshape(b, L, G, hg, P))
    dts = chunks(dt.reshape(b, L, G, hg))
    Bs, Cs = chunks(Bm), chunks(Cm)
    a_g = A.reshape(G, hg)
    causal = jnp.tril(jnp.ones((CHUNK, CHUNK), dtype=bool))[None, :, :, None, None]

    def step(h, inp):
        xc, dtc, Bc, Cc = inp
        acum = jnp.cumsum(dtc * a_g, axis=1)
        seg = acum[:, :, None] - acum[:, None, :]
        decay = jnp.exp(jnp.where(causal, seg, -jnp.inf))
        xdt = xc * dtc[..., None]
        cb = jnp.einsum('bign,bjgn->bijg', Cc, Bc)
        y = jnp.einsum('bijgh,bjghp->bighp', cb[..., None] * decay, xdt)
        y = y + jnp.einsum('bign,bghpn->bighp', Cc, h) * jnp.exp(acum)[..., None]
        tail = jnp.exp(acum[:, -1:] - acum)
        h = h * jnp.exp(acum[:, -1])[..., None, None] + jnp.einsum('bjgn,bjghp->bghpn', Bc, xdt * tail[..., None])
        return h, y

    h, ys = lax.scan(step, h0.reshape(b, G, hg, P, N), (xs, dts, Bs, Cs))
    return jnp.moveaxis(ys, 0, 1).reshape(b, L, H, P), h.reshape(b, H, P, N)


def ssd_mixer(u, conv_buf, h0, w_in, conv_w, conv_b, dt_bias, a_log, d_skip, norm_w, w_out):
    f32 = jnp.float32
    b, L, _ = u.shape
    z, xbc, dt = jnp.split(u @ w_in, [SSD_INNER, SSD_INNER + SSD_CONV_DIM], axis=-1)
    xbc, conv_new = causal_conv(xbc, conv_buf, conv_w, conv_b)
    xbc = jax.nn.silu(xbc)
    xs, Bm, Cm = jnp.split(xbc, [SSD_INNER, SSD_INNER + SSD_GROUPS * SSD_STATE], axis=-1)
    dt = jax.nn.softplus(dt.astype(f32) + dt_bias.astype(f32))
    A = -jnp.exp(a_log.astype(f32))
    pad = (-L) % CHUNK

    def padf(t):
        return jnp.pad(t.astype(f32), ((0, 0), (pad, 0)) + ((0, 0),) * (t.ndim - 2))

    xh = padf(xs.reshape(b, L, SSD_HEADS, SSD_HEADDIM))
    y, h_new = ssd_scan(xh, padf(dt), A,
                        padf(Bm.reshape(b, L, SSD_GROUPS, SSD_STATE)),
                        padf(Cm.reshape(b, L, SSD_GROUPS, SSD_STATE)), h0.astype(f32))
    y = y[:, pad:] + xh[:, pad:] * d_skip.astype(f32)[:, None]
    g = y.reshape(b, L, SSD_INNER) * jax.nn.silu(z.astype(f32))
    g = g.reshape(b, L, SSD_GROUPS, -1)
    g = g * lax.rsqrt(jnp.mean(g * g, axis=-1, keepdims=True) + EPS)
    g = g.reshape(b, L, SSD_INNER) * norm_w.astype(f32)
    return g.astype(u.dtype) @ w_out, conv_new, h_new.astype(h0.dtype)


def s5_scan(v, valid, lam_bar, b_bar, c_t, h0):
    b, L, G, S = v.shape
    nc = L // CHUNK
    vs = jnp.moveaxis(v.reshape(b, nc, CHUNK, G, S), 1, 0)
    ms = valid.reshape(nc, CHUNK)
    one = jnp.ones_like(lam_bar)

    def combine(e1, e2):
        a1, x1 = e1
        a2, x2 = e2
        return a1 * a2, a2 * x1 + x2

    def step(h, inp):
        vc, mc = inp
        bu = jnp.einsum('gps,bqgs->bqgp', b_bar, vc.astype(jnp.complex64))
        a = jnp.broadcast_to(jnp.where(mc[:, None, None], lam_bar, one), bu.shape)
        a_cum, x_cum = lax.associative_scan(combine, (a, bu), axis=1)
        states = a_cum * h[:, None] + x_cum
        y = jnp.einsum('gsp,bqgp->bqgs', c_t, states).real
        return states[:, -1], y

    h, ys = lax.scan(step, h0, (vs, ms))
    return jnp.moveaxis(ys, 0, 1).reshape(b, L, G, S), h


def s5_mixer(u, h0, w_in, lam_re, lam_im, log_step, b_re, b_im, c_re, c_im, d_skip, w_glu, b_glu, w_out):
    f32 = jnp.float32
    b, L, _ = u.shape
    v, gate = jnp.split(u @ w_in, 2, axis=-1)
    lam = lax.complex(lam_re.astype(f32), lam_im.astype(f32))
    delta = jnp.exp(log_step.astype(f32))[:, None]
    lam_bar = jnp.exp(lam * delta)
    b_bar = ((lam_bar - 1.0) / lam)[..., None] * lax.complex(b_re.astype(f32), b_im.astype(f32))
    c_t = lax.complex(c_re.astype(f32), c_im.astype(f32))
    pad = (-L) % CHUNK
    vg = jnp.pad(v.astype(f32).reshape(b, L, S5_GROUPS, S5_GROUP), ((0, 0), (pad, 0), (0, 0), (0, 0)))
    valid = jnp.arange(L + pad) >= pad
    y, h = s5_scan(vg, valid, lam_bar, b_bar, c_t, h0)
    y = y[:, pad:].reshape(b, L, S5_WIDTH) + v.astype(f32) * d_skip.astype(f32)
    y = jax.nn.gelu(y).astype(u.dtype)
    y = y * jax.nn.sigmoid(y @ w_glu + b_glu)
    return (y * jax.nn.silu(gate)) @ w_out, h


def rope_tables(pos):
    half = MLA_ROPE // 2
    inv = ROPE_THETA ** (-jnp.arange(half, dtype=jnp.float32) / half)
    ang = pos.astype(jnp.float32)[:, None] * inv[None, :]
    return jnp.cos(ang), jnp.sin(ang)


def apply_rope(x, cos, sin):
    half = MLA_ROPE // 2
    x1 = x[..., :half].astype(jnp.float32)
    x2 = x[..., half:].astype(jnp.float32)
    return jnp.concatenate([x1 * cos - x2 * sin, x2 * cos + x1 * sin], axis=-1).astype(x.dtype)


def mla_project(u, pos, w_in, q_norm_w, w_uq, kv_norm_w):
    b, L, _ = u.shape
    cq, ckv, kr, gate = jnp.split(u @ w_in, [MLA_Q_RANK, MLA_Q_RANK + MLA_KV_RANK,
                                             MLA_Q_RANK + MLA_KV_RANK + MLA_ROPE], axis=-1)
    q = (rms_norm(cq, q_norm_w) @ w_uq).reshape(b, L, MLA_HEADS, MLA_NOPE + MLA_ROPE)
    cos, sin = rope_tables(pos)
    q = jnp.concatenate([q[..., :MLA_NOPE], apply_rope(q[..., MLA_NOPE:], cos[:, None], sin[:, None])], axis=-1)
    return q, rms_norm(ckv, kv_norm_w), apply_rope(kr, cos, sin), gate


def block_attention(q, k, v, q_chunk, k_chunk):
    b, Lq, H, dk = q.shape
    dv = v.shape[-1]
    blk = min(QBLOCK, Lq)
    pad = (-Lq) % blk
    nb = (Lq + pad) // blk
    qb = jnp.moveaxis(jnp.pad(q, ((0, 0), (0, pad), (0, 0), (0, 0))).reshape(b, nb, blk, H, dk), 1, 0)
    cb = jnp.pad(q_chunk, (0, pad), constant_values=jnp.iinfo(jnp.int32).max).reshape(nb, blk)
    scale = dk ** -0.5

    def one_block(args):
        qi, ci = args
        s = jnp.einsum('bqhd,bkhd->bhqk', qi, k, preferred_element_type=jnp.float32) * scale
        s = jnp.where((k_chunk[None, :] <= ci[:, None])[None, None], s, -jnp.inf)
        p = jax.nn.softmax(s, axis=-1)
        return jnp.einsum('bhqk,bkhd->bqhd', p.astype(v.dtype), v, preferred_element_type=jnp.float32).astype(q.dtype)

    o = lax.map(one_block, (qb, cb))
    return jnp.moveaxis(o, 0, 1).reshape(b, nb * blk, H, dv)[:, :Lq]


def mla_attend(q, gate, ckv_all, kr_all, q_chunk, k_chunk, w_ukv, w_out):
    b, Lk, _ = ckv_all.shape
    kv = (ckv_all @ w_ukv).reshape(b, Lk, MLA_HEADS, MLA_NOPE + MLA_V)
    k = jnp.concatenate([kv[..., :MLA_NOPE],
                         jnp.broadcast_to(kr_all[:, :, None, :], (b, Lk, MLA_HEADS, MLA_ROPE))], axis=-1)
    o = block_attention(q, k, kv[..., MLA_NOPE:], q_chunk, k_chunk)
    o = o.reshape(b, q.shape[1], MLA_WIDTH) * jax.nn.silu(gate)
    return o @ w_out


def setup_inputs(seed: int = 0) -> dict:
    key = jax.random.key(seed)
    ks = iter(jax.random.split(key, 48))
    f32 = jnp.float32

    def nrm(shape, scale=1.0):
        return scale * jax.random.normal(next(ks), shape, f32)

    def uni(shape, lo, hi):
        return jax.random.uniform(next(ks), shape, f32, lo, hi)

    dt0 = jnp.exp(uni((N_SSD, SSD_HEADS), math.log(1e-3), math.log(1e-1)))
    return {
        'x_prompt': nrm((BATCH, SEQ, D_MODEL)),
        'x_sample': nrm((DEC_BATCH, DEC_SEQ, D_MODEL)),
        'state_ssd': nrm((N_SSD, DEC_BATCH, SSD_HEADS, SSD_HEADDIM, SSD_STATE), 0.1),
        'state_conv': nrm((N_SSD, DEC_BATCH, SSD_CONV - 1, SSD_CONV_DIM)),
        'state_s5': nrm((N_S5, DEC_BATCH, S5_GROUPS, S5_STATE, 2), 0.1),
        'cache_ckv': nrm((N_MLA, DEC_BATCH, PAST_LEN, MLA_KV_RANK)),
        'cache_kr': nrm((N_MLA, DEC_BATCH, PAST_LEN, MLA_ROPE)),
        'cache_meta_ckv': nrm((N_MLA, DEC_BATCH, N_META, MLA_KV_RANK)),
        'cache_meta_kr': nrm((N_MLA, DEC_BATCH, N_META, MLA_ROPE)),
        'meta_tokens': nrm((N_META, D_MODEL)),
        'norm_pre': 1.0 + nrm((DEPTH, D_MODEL), 0.02),
        'norm_post': 1.0 + nrm((DEPTH, D_MODEL), 0.02),
        'ssd_w_in': nrm((N_SSD, D_MODEL, SSD_IN_COLS), D_MODEL ** -0.5),
        'ssd_conv_w': nrm((N_SSD, SSD_CONV, SSD_CONV_DIM), SSD_CONV ** -0.5),
        'ssd_conv_b': nrm((N_SSD, SSD_CONV_DIM), 0.01),
        'ssd_dt_bias': dt0 + jnp.log(-jnp.expm1(-dt0)),
        'ssd_a_log': jnp.log(uni((N_SSD, SSD_HEADS), 1.0, 16.0)),
        'ssd_d': 1.0 + nrm((N_SSD, SSD_HEADS), 0.01),
        'ssd_norm': 1.0 + nrm((N_SSD, SSD_INNER), 0.02),
        'ssd_w_out': nrm((N_SSD, SSD_INNER, D_MODEL), SSD_INNER ** -0.5),
        's5_w_in': nrm((N_S5, D_MODEL, 2 * S5_WIDTH), D_MODEL ** -0.5),
        's5_lam_re': -0.5 + nrm((N_S5, S5_GROUPS, S5_STATE), 0.01),
        's5_lam_im': jnp.pi * jnp.arange(S5_STATE, dtype=f32) + nrm((N_S5, S5_GROUPS, S5_STATE), 0.01),
        's5_log_step': uni((N_S5, S5_GROUPS), math.log(1e-3), math.log(1e-1)),
        's5_b_re': nrm((N_S5, S5_GROUPS, S5_STATE, S5_GROUP), (2 * S5_GROUP) ** -0.5),
        's5_b_im': nrm((N_S5, S5_GROUPS, S5_STATE, S5_GROUP), (2 * S5_GROUP) ** -0.5),
        's5_c_re': nrm((N_S5, S5_GROUPS, S5_GROUP, S5_STATE), (2 * S5_STATE) ** -0.5),
        's5_c_im': nrm((N_S5, S5_GROUPS, S5_GROUP, S5_STATE), (2 * S5_STATE) ** -0.5),
        's5_d': nrm((N_S5, S5_WIDTH)),
        's5_w_glu': nrm((N_S5, S5_WIDTH, S5_WIDTH), S5_WIDTH ** -0.5),
        's5_b_glu': nrm((N_S5, S5_WIDTH), 0.01),
        's5_w_out': nrm((N_S5, S5_WIDTH, D_MODEL), S5_WIDTH ** -0.5),
        'mla_w_in': nrm((N_MLA, D_MODEL, MLA_IN_COLS), D_MODEL ** -0.5),
        'mla_q_norm': 1.0 + nrm((N_MLA, MLA_Q_RANK), 0.02),
        'mla_w_uq': nrm((N_MLA, MLA_Q_RANK, MLA_HEADS * (MLA_NOPE + MLA_ROPE)), MLA_Q_RANK ** -0.5),
        'mla_kv_norm': 1.0 + nrm((N_MLA, MLA_KV_RANK), 0.02),
        'mla_w_ukv': nrm((N_MLA, MLA_KV_RANK, MLA_HEADS * (MLA_NOPE + MLA_V)), MLA_KV_RANK ** -0.5),
        'mla_w_out': nrm((N_MLA, MLA_WIDTH, D_MODEL), MLA_WIDTH ** -0.5),
    }


def reference(x_prompt, x_sample, state_ssd, state_conv, state_s5, cache_ckv, cache_kr, cache_meta_ckv,
              cache_meta_kr, meta_tokens, norm_pre, norm_post, ssd_w_in, ssd_conv_w, ssd_conv_b, ssd_dt_bias,
              ssd_a_log, ssd_d, ssd_norm, ssd_w_out, s5_w_in, s5_lam_re, s5_lam_im, s5_log_step, s5_b_re,
              s5_b_im, s5_c_re, s5_c_im, s5_d, s5_w_glu, s5_b_glu, s5_w_out, mla_w_in, mla_q_norm, mla_w_uq,
              mla_kv_norm, mla_w_ukv, mla_w_out):
    f32 = jnp.float32
    dtype = x_prompt.dtype
    hp = jnp.concatenate([jnp.broadcast_to(meta_tokens.astype(dtype)[None], (BATCH, N_META, D_MODEL)), x_prompt], axis=1)
    hs = x_sample
    p_pos = jnp.arange(N_META + SEQ, dtype=jnp.int32)
    p_chunk = jnp.where(p_pos < N_META, 0, 1 + (p_pos - N_META) // CHUNK)
    s_frames = PAST_LEN + jnp.arange(DEC_SEQ, dtype=jnp.int32)
    s_chunk = 1 + s_frames // CHUNK
    s_kchunk = jnp.concatenate([jnp.zeros((N_META,), jnp.int32),
                                1 + jnp.arange(PAST_LEN, dtype=jnp.int32) // CHUNK, s_chunk])
    s_pos = N_META + s_frames

    ssd_p, ssd_s, conv_p, conv_s, s5_p, s5_s = [], [], [], [], [], []
    mckv_p, mkr_p, ckv_p, kr_p, ckv_s, kr_s = [], [], [], [], [], []
    for i in range(DEPTH):
        kind, j = i % N_MIXERS, i // N_MIXERS
        up = rms_norm(hp, norm_pre[i])
        us = rms_norm(hs, norm_pre[i])
        if kind == 0:
            w = (ssd_w_in[j], ssd_conv_w[j], ssd_conv_b[j], ssd_dt_bias[j], ssd_a_log[j], ssd_d[j], ssd_norm[j], ssd_w_out[j])
            op, cp, sp = ssd_mixer(up, jnp.zeros((BATCH, SSD_CONV - 1, SSD_CONV_DIM), dtype),
                                   jnp.zeros((BATCH, SSD_HEADS, SSD_HEADDIM, SSD_STATE), state_ssd.dtype), *w)
            os_, cs, ss = ssd_mixer(us, state_conv[j], state_ssd[j], *w)
            ssd_p.append(sp)
            ssd_s.append(ss)
            conv_p.append(cp)
            conv_s.append(cs)
        elif kind == 1:
            w = (s5_w_in[j], s5_lam_re[j], s5_lam_im[j], s5_log_step[j], s5_b_re[j], s5_b_im[j], s5_c_re[j],
                 s5_c_im[j], s5_d[j], s5_w_glu[j], s5_b_glu[j], s5_w_out[j])
            op, hpp = s5_mixer(up, jnp.zeros((BATCH, S5_GROUPS, S5_STATE), jnp.complex64), *w)
            st = state_s5[j]
            os_, hss = s5_mixer(us, lax.complex(st[..., 0].astype(f32), st[..., 1].astype(f32)), *w)
            s5_p.append(jnp.stack([hpp.real, hpp.imag], axis=-1).astype(state_s5.dtype))
            s5_s.append(jnp.stack([hss.real, hss.imag], axis=-1).astype(state_s5.dtype))
        else:
            q, ckv, kr, gate = mla_project(up, p_pos, mla_w_in[j], mla_q_norm[j], mla_w_uq[j], mla_kv_norm[j])
            op = mla_attend(q, gate, ckv, kr, p_chunk, p_chunk, mla_w_ukv[j], mla_w_out[j])
            qs, ckvs, krs, gs = mla_project(us, s_pos, mla_w_in[j], mla_q_norm[j], mla_w_uq[j], mla_kv_norm[j])
            ckv_all = jnp.concatenate([cache_meta_ckv[j].astype(dtype), cache_ckv[j].astype(dtype), ckvs], axis=1)
            kr_all = jnp.concatenate([cache_meta_kr[j].astype(dtype), cache_kr[j].astype(dtype), krs], axis=1)
            os_ = mla_attend(qs, gs, ckv_all, kr_all, s_chunk, s_kchunk, mla_w_ukv[j], mla_w_out[j])
            mckv_p.append(ckv[:, :N_META])
            mkr_p.append(kr[:, :N_META])
            ckv_p.append(ckv[:, N_META:])
            kr_p.append(kr[:, N_META:])
            ckv_s.append(ckvs)
            kr_s.append(krs)
        hp = hp + rms_norm(op, norm_post[i])
        hs = hs + rms_norm(os_, norm_post[i])

    y_prompt = hp[:, N_META:]
    y_sample = hs
    return (y_prompt, y_sample, jnp.stack(ssd_p), jnp.stack(ssd_s), jnp.stack(conv_p), jnp.stack(conv_s),
            jnp.stack(s5_p), jnp.stack(s5_s), jnp.stack(mckv_p), jnp.stack(mkr_p), jnp.stack(ckv_p),
            jnp.stack(kr_p), jnp.stack(ckv_s), jnp.stack(kr_s))
```

```python
import functools
import math

import numpy as np
import jax
import jax.numpy as jnp
from jax import lax
from jax.experimental import pallas as pl
from jax.experimental.pallas import tpu as pltpu

F32 = jnp.float32
BF16 = jnp.bfloat16

EPS = 1e-6
CHUNK = 64
ROPE_THETA = 10000.0

LANE = 128
SUBLANE = 8
VMEM_CAP = 56 * 1024 * 1024
NEG = -1e30
S5_T = 8


def _tile(dim, target, align=SUBLANE):
    best = None
    for t in range(align, min(dim, target) + 1, align):
        if dim % t == 0:
            best = t
    assert best is not None, (dim, target, align)
    return best


def _params(sem, vmem_bytes):
    limit = int(min(VMEM_CAP, max(vmem_bytes * 5 // 4 + (4 << 20), 16 << 20)))
    return pltpu.CompilerParams(dimension_semantics=sem, vmem_limit_bytes=limit)


def _sigmoid(x):
    return 1.0 / (1.0 + jnp.exp(-x))


def _silu(x):
    return x * _sigmoid(x)


def _softplus(x):
    return jnp.maximum(x, 0.0) + jnp.log1p(jnp.exp(-jnp.abs(x)))


def _gelu_tanh(x):
    return 0.5 * x * (1.0 + jnp.tanh(math.sqrt(2.0 / math.pi) * (x + 0.044715 * (x * x * x))))


def _split3(x):
    hi = x.astype(BF16)
    r1 = x - hi.astype(F32)
    mid = r1.astype(BF16)
    lo = (r1 - mid.astype(F32)).astype(BF16)
    return hi, mid, lo


def _dot(a, b):
    return jnp.dot(a, b, preferred_element_type=F32)


def _dot_f32_lhs(x, e):
    hi, mid, lo = _split3(x)
    return _dot(hi, e) + _dot(mid, e) + _dot(lo, e)


def _dot_f32_rhs(e, x):
    hi, mid, lo = _split3(x)
    return _dot(e, hi) + _dot(e, mid) + _dot(e, lo)


def _mm_kernel(*refs, nk, n_extra, epilogue):
    x_ref, w_ref = refs[0], refs[1]
    extra = refs[2:2 + n_extra]
    o_ref = refs[2 + n_extra]
    wb_ref = refs[3 + n_extra]
    i = pl.program_id(1)
    k = pl.program_id(2)

    @pl.when(i == 0)
    def _():
        wb_ref[k] = w_ref[...].astype(BF16)

    part = _dot(x_ref[...], wb_ref[k])

    def finish(acc):
        vals = [e[...] for e in extra]
        o_ref[...] = (epilogue(acc, *vals) if epilogue is not None else acc).astype(o_ref.dtype)

    if nk == 1:
        finish(part)
    else:
        acc_ref = refs[4 + n_extra]

        @pl.when(k == 0)
        def _():
            acc_ref[...] = part

        @pl.when(k > 0)
        def _():
            acc_ref[...] += part

        @pl.when(k == nk - 1)
        def _():
            finish(acc_ref[...])


def _mm(x, w, *, n_cols, col_off=0, out_dtype, rows=None, tn=512, tm_target=1024, tk_target=4096,
        epilogue=None, extras=(), name):
    m = x.shape[0] if rows is None else rows
    kdim = x.shape[1]
    assert w.shape[0] == kdim and x.dtype == BF16
    tn = min(tn, n_cols)
    assert n_cols % tn == 0 and col_off % tn == 0, (n_cols, col_off, tn)
    tm = _tile(m, tm_target, 16)
    tk = _tile(kdim, tk_target, LANE)
    nk = kdim // tk
    cb = col_off // tn
    grid = (n_cols // tn, m // tm, nk)

    def w_map(j, i, k):
        return (jnp.where(i == 0, k, nk - 1), j + cb)

    in_specs = [pl.BlockSpec((tm, tk), lambda j, i, k: (i, k)),
                pl.BlockSpec((tk, tn), w_map)]
    args = [x, w]
    for arr, kind, off in extras:
        ob = off // tn
        assert off % tn == 0
        if kind == 'tile':
            in_specs.append(pl.BlockSpec((tm, tn), lambda j, i, k, ob=ob: (i, j + ob)))
        else:
            in_specs.append(pl.BlockSpec((1, tn), lambda j, i, k, ob=ob: (0, j + ob)))
        args.append(arr)
    scratch = [pltpu.VMEM((nk, tk, tn), BF16)]
    if nk > 1:
        scratch.append(pltpu.VMEM((tm, tn), F32))
    osz = jnp.dtype(out_dtype).itemsize
    vmem = (2 * tm * tk * 2 + 2 * tk * tn * 4 + nk * tk * tn * 2 + tm * tn * 4 * 2 + 2 * tm * tn * osz
            + sum(2 * tm * tn * a.dtype.itemsize for a, kind, _ in extras if kind == 'tile'))
    return pl.pallas_call(
        functools.partial(_mm_kernel, nk=nk, n_extra=len(extras), epilogue=epilogue),
        grid=grid,
        in_specs=in_specs,
        out_specs=pl.BlockSpec((tm, tn), lambda j, i, k: (i, j)),
        out_shape=jax.ShapeDtypeStruct((m, n_cols), out_dtype),
        scratch_shapes=scratch,
        compiler_params=_params(("parallel", "arbitrary", "arbitrary"), vmem),
        name=name,
    )(*args)


def _rms(x, w):
    return x * lax.rsqrt(jnp.mean(x * x, axis=-1, keepdims=True) + EPS) * w


def _prenorm_kernel(h_ref, w_ref, u_ref):
    u_ref[...] = _rms(h_ref[...], w_ref[...]).astype(u_ref.dtype)


def _prenorm(h, w_pre):
    m, d = h.shape
    tr = _tile(m, 256)
    return pl.pallas_call(
        _prenorm_kernel,
        grid=(m // tr,),
        in_specs=[pl.BlockSpec((tr, d), lambda i: (i, 0)), pl.BlockSpec((1, d), lambda i: (0, 0))],
        out_specs=pl.BlockSpec((tr, d), lambda i: (i, 0)),
        out_shape=jax.ShapeDtypeStruct((m, d), BF16),
        compiler_params=_params(("parallel",), 2 * tr * d * 6),
        name="prenorm",
    )(h, w_pre.reshape(1, d))


def _postnorm_kernel(h_ref, o_ref, wpost_ref, wpre_ref, hout_ref, u_ref, *, tr, n_pad):
    rows = pl.program_id(0) * tr + lax.broadcasted_iota(jnp.int32, (tr, 1), 0)
    h = h_ref[...] + _rms(o_ref[...].astype(F32), wpost_ref[...])
    h = jnp.where(rows >= n_pad, h, 0.0)
    hout_ref[...] = h
    u_ref[...] = _rms(h, wpre_ref[...]).astype(u_ref.dtype)


def _postnorm(h, o, w_post, w_pre, n_pad):
    m, d = h.shape
    tr = _tile(m, 256)
    row = pl.BlockSpec((tr, d), lambda i: (i, 0))
    vec = pl.BlockSpec((1, d), lambda i: (0, 0))
    return pl.pallas_call(
        functools.partial(_postnorm_kernel, tr=tr, n_pad=n_pad),
        grid=(m // tr,),
        in_specs=[row, row, vec, vec],
        out_specs=[row, row],
        out_shape=[jax.ShapeDtypeStruct((m, d), F32), jax.ShapeDtypeStruct((m, d), BF16)],
        compiler_params=_params(("parallel",), 2 * tr * d * (4 + 4 + 4 + 2)),
        name="postnorm",
    )(h, o, w_post.reshape(1, d), w_pre.reshape(1, d))


def _ssd_kernel(z_ref, x_ref, b_ref, c_ref, cwx_ref, cwb_ref, cwc_ref, cbx_ref, cbb_ref, cbc_ref,
                bufx_ref, bufb_ref, bufc_ref, dtc_ref, dtr_ref, dtbc_ref, dtbr_ref, ac_ref, ar_ref,
                dsk_ref, nw_ref, h0_ref, o_ref, hT_ref, xext, bext, cext, st, ybuf,
                *, q, hg, p, n_pad):
    c = pl.program_id(2)
    gw = hg * p

    @pl.when(c == 0)
    def _():
        xext[0:SUBLANE, :] = bufx_ref[...]
        bext[0:SUBLANE, :] = bufb_ref[...]
        cext[0:SUBLANE, :] = bufc_ref[...]
        st[...] = h0_ref[...]

    def conv_silu(ext, raw_ref, w_ref, bias_ref):
        ext[SUBLANE:SUBLANE + q, :] = raw_ref[...]
        acc = bias_ref[...]
        for tap in range(4):
            acc = acc + w_ref[tap:tap + 1, :] * ext[SUBLANE - 3 + tap:SUBLANE - 3 + tap + q, :]
        ext[0:SUBLANE, :] = ext[q:q + SUBLANE, :]
        return _silu(acc)

    xs = conv_silu(xext, x_ref, cwx_ref, cbx_ref)
    bc = conv_silu(bext, b_ref, cwb_ref, cbb_ref)
    cc = conv_silu(cext, c_ref, cwc_ref, cbc_ref)

    rows_c = c * q + lax.broadcasted_iota(jnp.int32, (q, 1), 0)
    rows_r = c * q + lax.broadcasted_iota(jnp.int32, (1, q), 1)
    dt_c = jnp.where(rows_c >= n_pad, _softplus(dtc_ref[...] + dtbc_ref[...]), 0.0)
    dt_r = jnp.where(rows_r >= n_pad, _softplus(dtr_ref[...] + dtbr_ref[...]), 0.0)
    ii = lax.broadcasted_iota(jnp.int32, (q, q), 0)
    jj = lax.broadcasted_iota(jnp.int32, (q, q), 1)
    tri = jj <= ii
    tril = jnp.where(tri, 1.0, 0.0).astype(BF16)
    triu = jnp.where(ii <= jj, 1.0, 0.0).astype(BF16)
    acum = _dot_f32_rhs(tril, dt_c * ac_ref[...])
    acum_t = _dot_f32_lhs(dt_r * ar_ref[...], triu)

    eh = lax.broadcasted_iota(jnp.int32, (LANE, gw), 0)
    el = lax.broadcasted_iota(jnp.int32, (LANE, gw), 1)
    expand = jnp.where(el // p == eh, 1.0, 0.0).astype(BF16)
    dt_x = _dot_f32_lhs(dt_c, expand)
    acum_x = _dot_f32_lhs(acum, expand)
    last_x = acum_x[q - 1:q, :]
    decay_in = jnp.exp(acum_x)
    decay_out = jnp.exp(last_x - acum_x)
    decay_all = jnp.exp(last_x)

    xdt = xs * dt_x
    xdt_b = xdt.astype(BF16)
    xw_b = (xdt * decay_out).astype(BF16)
    bc_t = bc.T.astype(BF16)
    cc_b = cc.astype(BF16)
    cb = _dot(cc_b, bc_t)
    lane_head = lax.broadcasted_iota(jnp.int32, (q, LANE), 1) // p
    heads_per_slab = LANE // p

    for s in range(gw // LANE):
        sl = slice(s * LANE, (s + 1) * LANE)
        y = _dot(cc_b, st[:, sl].astype(BF16)) * decay_in[:, sl]
        xd = xdt_b[:, sl]
        for r in range(heads_per_slab):
            hh = s * heads_per_slab + r
            seg = acum[:, hh:hh + 1] - acum_t[hh:hh + 1, :]
            dec = jnp.exp(jnp.where(tri, seg, NEG))
            mh = (cb * dec).astype(BF16)
            y = y + _dot(mh, jnp.where(lane_head == r, xd, jnp.zeros_like(xd)))
        st[:, sl] = st[:, sl] * decay_all[:, sl] + _dot(bc_t, xw_b[:, sl])
        ybuf[:, sl] = y + xs[:, sl] * dsk_ref[:, sl]

    g = ybuf[...] * _silu(z_ref[...])
    g = g * lax.rsqrt(jnp.mean(g * g, axis=-1, keepdims=True) + EPS)
    o_ref[...] = (g * nw_ref[...]).astype(o_ref.dtype)
    hT_ref[...] = st[...]


def _ssd_scan(zx, dt_c, dt_r, prm, conv_buf8, h0_t, *, q, nseq, nchunks, row0, n_pad, out_prev, m, name):
    g_, hg, p, n, inner = prm['G'], prm['hg'], prm['P'], prm['N'], prm['inner']
    gw = hg * p
    hgp = dt_r.shape[2]
    assert row0 % q == 0
    rb0 = row0 // q
    xo, bo, co = inner // gw, (2 * inner) // n, (2 * inner + g_ * n) // n
    cbo, cco = inner // n, (inner + g_ * n) // n

    def rmap(s, g, c):
        return rb0 + s * nchunks + c

    in_specs = [
        pl.BlockSpec((q, gw), lambda s, g, c: (rmap(s, g, c), g)),
        pl.BlockSpec((q, gw), lambda s, g, c: (rmap(s, g, c), xo + g)),
        pl.BlockSpec((q, n), lambda s, g, c: (rmap(s, g, c), bo + g)),
        pl.BlockSpec((q, n), lambda s, g, c: (rmap(s, g, c), co + g)),
        pl.BlockSpec((4, gw), lambda s, g, c: (0, g)),
        pl.BlockSpec((4, n), lambda s, g, c: (0, cbo + g)),
        pl.BlockSpec((4, n), lambda s, g, c: (0, cco + g)),
        pl.BlockSpec((1, gw), lambda s, g, c: (0, g)),
        pl.BlockSpec((1, n), lambda s, g, c: (0, cbo + g)),
        pl.BlockSpec((1, n), lambda s, g, c: (0, cco + g)),
        pl.BlockSpec((None, SUBLANE, gw), lambda s, g, c: (s, 0, g)),
        pl.BlockSpec((None, SUBLANE, n), lambda s, g, c: (s, 0, cbo + g)),
        pl.BlockSpec((None, SUBLANE, n), lambda s, g, c: (s, 0, cco + g)),
        pl.BlockSpec((None, q, LANE), lambda s, g, c: (g, rmap(s, g, c), 0)),
        pl.BlockSpec((None, None, hgp, q), lambda s, g, c: (g, s * nchunks + c, 0, 0)),
        pl.BlockSpec((None, 1, LANE), lambda s, g, c: (g, 0, 0)),
        pl.BlockSpec((None, hgp, 1), lambda s, g, c: (g, 0, 0)),
        pl.BlockSpec((None, 1, LANE), lambda s, g, c: (g, 0, 0)),
        pl.BlockSpec((None, hgp, 1), lambda s, g, c: (g, 0, 0)),
        pl.BlockSpec((1, gw), lambda s, g, c: (0, g)),
        pl.BlockSpec((1, gw), lambda s, g, c: (0, g)),
        pl.BlockSpec((None, None, n, gw), lambda s, g, c: (s, g, 0, 0)),
    ]
    args = [zx, zx, zx, zx, prm['conv_w'], prm['conv_w'], prm['conv_w'], prm['conv_b'], prm['conv_b'],
            prm['conv_b'], conv_buf8, conv_buf8, conv_buf8, dt_c, dt_r, prm['dtb_c'], prm['dtb_r'],
            prm['a_c'], prm['a_r'], prm['d_x'], prm['norm_w'], h0_t]
    aliases = {}
    if out_prev is not None:
        in_specs.append(pl.BlockSpec(memory_space=pl.ANY))
        args.append(out_prev)
        aliases = {len(args) - 1: 0}

    def body(*refs):
        if out_prev is not None:
            refs = refs[:22] + refs[23:]
        _ssd_kernel(*refs, q=q, hg=hg, p=p, n_pad=n_pad)

    vmem = (2 * (2 * q * gw * 4 + 2 * q * n * 4 + q * gw * 2 + 2 * n * gw * 4) + (2 * q + 8) * gw * 4
            + n * gw * 4 + 24 * q * gw * 4)
    return pl.pallas_call(
        body,
        grid=(nseq, g_, nchunks),
        in_specs=in_specs,
        out_specs=[pl.BlockSpec((q, gw), lambda s, g, c: (rmap(s, g, c), g)),
                   pl.BlockSpec((None, None, n, gw), lambda s, g, c: (s, g, 0, 0))],
        out_shape=[jax.ShapeDtypeStruct((m, inner), BF16),
                   jax.ShapeDtypeStruct((nseq, g_, n, gw), F32)],
        scratch_shapes=[pltpu.VMEM((q + SUBLANE, gw), F32), pltpu.VMEM((q + SUBLANE, n), F32),
                        pltpu.VMEM((q + SUBLANE, n), F32), pltpu.VMEM((n, gw), F32),
                        pltpu.VMEM((q, gw), F32)],
        input_output_aliases=aliases,
        compiler_params=_params(("parallel", "parallel", "arbitrary"), vmem),
        name=name,
    )(*args)


def _ssd_layer(u, lay, w_in, conv_w, conv_b, dt_bias, a_log, d_skip, norm_w, w_out, conv_state, ssd_state):
    m = u.shape[0]
    lp, nb, ls, n_pad = lay['lp'], lay['nb'], lay['ls'], lay['n_pad']
    h_, = dt_bias.shape
    inner = w_out.shape[0]
    p = inner // h_
    n = ssd_state.shape[-1]
    conv_dim = conv_w.shape[1]
    g_ = (conv_dim - inner) // (2 * n)
    hg = h_ // g_
    gw = hg * p
    assert LANE % p == 0 and gw % LANE == 0 and n % LANE == 0 and hg <= LANE
    hgp = -(-hg // SUBLANE) * SUBLANE

    zx = _mm(u, w_in, n_cols=2 * inner + 2 * g_ * n, out_dtype=F32, tn=_tile(2 * inner + 2 * g_ * n, 512, LANE),
             name="ssd_in")
    w_dt = w_in[:, inner + conv_dim:].reshape(-1, g_, hg)
    w_dt = jnp.pad(w_dt, ((0, 0), (0, 0), (0, LANE - hg))).reshape(-1, g_ * LANE)
    dt_raw = _mm(u, w_dt, n_cols=g_ * LANE, out_dtype=F32, tn=LANE, name="ssd_dt")
    dt_c = dt_raw.reshape(m, g_, LANE).transpose(1, 0, 2)

    def dt_rows(q, start, stop):
        return dt_c[:, start:stop, :hgp].reshape(g_, (stop - start) // q, q, hgp).transpose(0, 1, 3, 2)

    a = -jnp.exp(a_log.astype(F32)).reshape(g_, hg)
    dtb = dt_bias.astype(F32).reshape(g_, hg)
    prm = dict(
        G=g_, hg=hg, P=p, N=n, inner=inner,
        conv_w=conv_w, conv_b=conv_b.reshape(1, conv_dim),
        dtb_c=jnp.pad(dtb, ((0, 0), (0, LANE - hg)))[:, None, :],
        dtb_r=jnp.pad(dtb, ((0, 0), (0, hgp - hg)))[:, :, None],
        a_c=jnp.pad(a, ((0, 0), (0, LANE - hg)))[:, None, :],
        a_r=jnp.pad(a, ((0, 0), (0, hgp - hg)))[:, :, None],
        d_x=jnp.repeat(d_skip.astype(F32), p).reshape(1, inner),
        norm_w=norm_w.reshape(1, inner),
    )

    def buf8(buf):
        return jnp.pad(buf.astype(F32), ((0, 0), (SUBLANE - 3, 0), (0, 0)))

    def state_t(h):
        b = h.shape[0]
        return h.astype(F32).reshape(b, g_, hg, p, n).transpose(0, 1, 4, 2, 3).reshape(b, g_, n, gw)

    def state_back(ht):
        b = ht.shape[0]
        return ht.reshape(b, g_, n, hg, p).transpose(0, 1, 3, 4, 2).reshape(b, h_, p, n)

    qp = 128
    y, hp_t = _ssd_scan(zx, dt_c, dt_rows(qp, 0, lp), prm, jnp.zeros((1, SUBLANE, conv_dim), F32),
                        jnp.zeros((1, g_, n, gw), F32), q=qp, nseq=1, nchunks=lp // qp, row0=0,
                        n_pad=n_pad, out_prev=None, m=m, name="ssd_scan_prompt")
    y, hs_t = _ssd_scan(zx, dt_c, dt_rows(ls, lp, m), prm, buf8(conv_state), state_t(ssd_state), q=ls, nseq=nb,
                        nchunks=1, row0=lp, n_pad=0, out_prev=y, m=m, name="ssd_scan_sample")
    o = _mm(y, w_out, n_cols=w_out.shape[1], out_dtype=F32, name="ssd_out")

    xbc = zx[:, inner:]
    conv_p = xbc[lp - 3:lp][None]
    conv_s = xbc[lp:].reshape(nb, ls, conv_dim)[:, ls - 3:]
    return o, state_back(hp_t), state_back(hs_t), conv_p, conv_s


def _cmul(pr, pi, xr, xi):
    return pr * xr - pi * xi, pr * xi + pi * xr


def _s5_kernel(v_ref, um_ref, km_ref, cm_ref, p_ref, d_ref, h0_ref, y_ref, hout_ref, carry, yscr,
               *, rb, seg, nseg, sw):
    t_ = S5_T

    @pl.when(pl.program_id(1) == 0)
    def _():
        carry[...] = h0_ref[...]

    vs = [v_ref[pl.ds(i, rb, stride=t_), :] for i in range(t_)]
    vcat = jnp.concatenate(vs, axis=1).astype(BF16)
    x = _dot(vcat, um_ref[...])
    rowk = lax.broadcasted_iota(jnp.int32, (rb, 1), 0) % seg

    def shifted(a, s):
        return jnp.where(rowk >= s, pltpu.roll(a, s, axis=0), 0.0)

    s = 1
    while s < seg:
        sh = shifted(x, s)
        ar, ai = _cmul(p_ref[s:s + 1, :sw], p_ref[s:s + 1, sw:], sh[:, :sw], sh[:, sw:])
        x = x + jnp.concatenate([ar, ai], axis=1)
        s *= 2

    hin = carry[...]
    if nseg == 1:
        hin_rows = jnp.broadcast_to(hin, (rb, 2 * sw))
        ptab = p_ref[0:seg, :]
    else:
        hin_rows = jnp.broadcast_to(hin[:, None, :], (nseg, seg, 2 * sw)).reshape(rb, 2 * sw)
        ptab = jnp.concatenate([p_ref[0:seg, :]] * nseg, axis=0)
    cr, ci = _cmul(ptab[:, :sw], ptab[:, sw:], hin_rows[:, :sw], hin_rows[:, sw:])
    hb = shifted(x, 1) + jnp.concatenate([cr, ci], axis=1)

    y_all = _dot(vcat, km_ref[...]) + _dot(hb.astype(BF16), cm_ref[...])
    for j in range(t_):
        yj = y_all[:, j * LANE:(j + 1) * LANE] + vs[j] * d_ref[...]
        yscr[pl.ds(j, rb, stride=t_), :] = _gelu_tanh(yj)
    y_ref[...] = yscr[...].astype(y_ref.dtype)

    if nseg == 1:
        xlast = x[seg - 1:seg, :]
    else:
        pick = (lax.broadcasted_iota(jnp.int32, (nseg, rb), 1)
                == lax.broadcasted_iota(jnp.int32, (nseg, rb), 0) * seg + (seg - 1))
        xlast = _dot_f32_rhs(jnp.where(pick, 1.0, 0.0).astype(BF16), x)
    nr, ni = _cmul(p_ref[seg:seg + 1, :sw], p_ref[seg:seg + 1, sw:], hin[:, :sw], hin[:, sw:])
    new = xlast + jnp.concatenate([nr, ni], axis=1)
    carry[...] = new
    hout_ref[...] = new


def _s5_scan(vg, mats, ptab, d_row, h0, *, rb, seg, nseg, nblocks, m, name):
    um, km, cm = mats
    nt, tl, sw2 = um.shape
    sw = sw2 // 2
    width = nt * LANE
    rows = rb * S5_T
    in_specs = [
        pl.BlockSpec((rows, LANE), lambda t, r: (r, t)),
        pl.BlockSpec((None, tl, sw2), lambda t, r: (t, 0, 0)),
        pl.BlockSpec((None, tl, tl), lambda t, r: (t, 0, 0)),
        pl.BlockSpec((None, sw2, tl), lambda t, r: (t, 0, 0)),
        pl.BlockSpec((None, ptab.shape[1], sw2), lambda t, r: (t, 0, 0)),
        pl.BlockSpec((1, LANE), lambda t, r: (0, t)),
        pl.BlockSpec((None, nseg, sw2), lambda t, r: (t, 0, 0)),
    ]
    vmem = 2 * (rows * LANE * 6 + tl * sw2 * 4 + tl * tl * 2 + ptab.shape[1] * sw2 * 4) + 12 * rb * sw2 * 4
    return pl.pallas_call(
        functools.partial(_s5_kernel, rb=rb, seg=seg, nseg=nseg, sw=sw),
        grid=(nt, nblocks),
        in_specs=in_specs,
        out_specs=[pl.BlockSpec((rows, LANE), lambda t, r: (r, t)),
                   pl.BlockSpec((None, nseg, sw2), lambda t, r: (t, 0, 0))],
        out_shape=[jax.ShapeDtypeStruct((m, width), BF16),
                   jax.ShapeDtypeStruct((nt, nseg, sw2), F32)],
        scratch_shapes=[pltpu.VMEM((nseg, sw2), F32), pltpu.VMEM((rows, LANE), F32)],
        compiler_params=_params(("parallel", "arbitrary"), vmem),
        name=name,
    )(vg, um, km, cm, ptab, d_row, h0)


def _s5_tables(lam_re, lam_im, log_step, b_re, b_im, c_re, c_im, max_seg):
    t_ = S5_T
    g_, p = lam_re.shape
    s_ = b_re.shape[-1]
    gl = LANE // s_
    nt = g_ // gl
    lam = lax.complex(lam_re.astype(F32), lam_im.astype(F32))
    delta = jnp.exp(log_step.astype(F32))[:, None]
    lam_bar = jnp.exp(lam * delta)
    b_bar = ((lam_bar - 1.0) / lam)[..., None] * lax.complex(b_re.astype(F32), b_im.astype(F32))
    c_t = lax.complex(c_re.astype(F32), c_im.astype(F32))

    def power(d):
        d = jnp.asarray(d, F32)[..., None, None]
        return jnp.exp((lam * delta) * d)

    eye = jnp.eye(gl, dtype=F32)

    pw = power(jnp.arange(t_ + 1))
    ub = pw[t_ - 1 - jnp.arange(t_)][:, :, :, None] * b_bar[None]
    ub = ub.transpose(1, 0, 3, 2).reshape(nt, gl, t_, s_, p)

    def u_part(x):
        y = x[:, :, :, :, None, :] * eye[None, :, None, None, :, None]
        return y.transpose(0, 2, 1, 3, 4, 5).reshape(nt, t_ * gl * s_, gl * p)

    um = jnp.concatenate([u_part(ub.real), u_part(ub.imag)], axis=-1).astype(BF16)

    kd = jnp.einsum('gop,dgp,gpi->dgio', c_t, pw[:t_], b_bar).real
    ii = jnp.arange(t_)[:, None]
    jj = jnp.arange(t_)[None, :]
    kf = jnp.where((jj >= ii)[:, :, None, None, None], kd[jnp.clip(jj - ii, 0, t_ - 1)], 0.0)
    kf = kf.transpose(2, 0, 3, 1, 4).reshape(nt, gl, t_, s_, t_, s_)
    km = kf[:, :, :, :, :, None, :] * eye[None, :, None, None, None, :, None]
    km = km.transpose(0, 2, 1, 3, 4, 5, 6).reshape(nt, t_ * gl * s_, t_ * gl * s_).astype(BF16)

    cw = c_t[None] * pw[1:][:, :, None, :]
    cw = cw.transpose(1, 3, 0, 2).reshape(nt, gl, p, t_, s_)

    def c_part(x):
        y = x[:, :, :, :, None, :] * eye[None, :, None, None, :, None]
        return y.reshape(nt, gl * p, t_ * gl * s_)

    cm = jnp.concatenate([c_part(cw.real), c_part(-cw.imag)], axis=1).astype(BF16)

    nrow = -(-(max_seg + 1) // SUBLANE) * SUBLANE
    pt = power(t_ * jnp.arange(nrow))
    pt = pt.reshape(nrow, nt, gl * p).transpose(1, 0, 2)
    ptab = jnp.concatenate([pt.real, pt.imag], axis=-1)
    return (um, km, cm), ptab


def _s5_layer(u, lay, w_in, lam_re, lam_im, log_step, b_re, b_im, c_re, c_im, d_skip, w_glu, b_glu, w_out,
              state):
    m = u.shape[0]
    lp, nb, ls = lay['lp'], lay['nb'], lay['ls']
    width = w_glu.shape[0]
    g_, p = lam_re.shape
    s_ = b_re.shape[-1]
    gl = LANE // s_
    nt = g_ // gl
    sw = gl * p
    assert g_ * s_ == width and LANE % s_ == 0 and ls % S5_T == 0 and lp % S5_T == 0

    vg = _mm(u, w_in, n_cols=2 * width, out_dtype=F32, name="s5_in")
    rb_p = _tile(lp // S5_T, 384)
    seg_s = ls // S5_T
    mats, ptab = _s5_tables(lam_re, lam_im, log_step, b_re, b_im, c_re, c_im, max(rb_p, seg_s))
    d_row = d_skip.astype(F32).reshape(1, width)

    def to_tiles(st):
        b = st.shape[0]
        return st.astype(F32).reshape(b, nt, gl, p, 2).transpose(1, 0, 4, 2, 3).reshape(nt, b, 2 * sw)

    def from_tiles(h):
        b = h.shape[1]
        return h.reshape(nt, b, 2, gl, p).transpose(1, 0, 3, 4, 2).reshape(b, g_, p, 2)

    y, hp = _s5_scan(vg, mats, ptab, d_row, jnp.zeros((nt, 1, 2 * sw), F32), rb=rb_p, seg=rb_p, nseg=1,
                     nblocks=lp // (rb_p * S5_T), m=m, name="s5_scan_prompt")
    y_s, hs = _s5_scan(vg[lp:], mats, ptab, d_row, to_tiles(state), rb=nb * seg_s, seg=seg_s, nseg=nb,
                       nblocks=1, m=nb * ls, name="s5_scan_sample")
    y = lax.dynamic_update_slice(y, y_s, (lp, 0))

    def glu(acc, yv, gate, bias):
        return yv.astype(F32) * _sigmoid(acc + bias) * _silu(gate)

    t = _mm(y, w_glu, n_cols=width, out_dtype=BF16, epilogue=glu,
            extras=((y, 'tile', 0), (vg, 'tile', width), (b_glu.reshape(1, width), 'row', 0)), name="s5_glu")
    o = _mm(t, w_out, n_cols=w_out.shape[1], out_dtype=F32, name="s5_out")
    return o, from_tiles(hp), from_tiles(hs)


def _rope_tile(t, cos_ref, sina_ref, sinb_ref, half):
    return (t * cos_ref[...] + pltpu.roll(t, LANE - half, axis=1) * sina_ref[...]
            + pltpu.roll(t, half, axis=1) * sinb_ref[...])


def _mla_prep_kernel(c_ref, qw_ref, kvw_ref, cos_ref, sina_ref, sinb_ref,
                     cqn_ref, ckv_ref, kr_ref, ckr_ref, *, qr, kvr, rope):
    c = c_ref[...]
    cqn_ref[...] = _rms(c[:, :qr], qw_ref[...]).astype(cqn_ref.dtype)
    ckv = _rms(c[:, qr:qr + kvr], kvw_ref[...])
    ckv_ref[...] = ckv
    rot = _rope_tile(c[:, qr + kvr:qr + kvr + LANE], cos_ref, sina_ref, sinb_ref, rope // 2)
    kr_ref[...] = rot[:, :rope]
    ckr_ref[...] = jnp.concatenate([ckv, rot], axis=1).astype(ckr_ref.dtype)


def _mla_prep(gc, col_off, wc, q_norm, kv_norm, tabs, *, qr, kvr, rope):
    m = gc.shape[0]
    tr = _tile(m, 256)
    assert col_off % wc == 0 and wc >= qr + kvr + LANE
    cb = col_off // wc
    row = lambda w: pl.BlockSpec((tr, w), lambda i: (i, 0))
    return pl.pallas_call(
        functools.partial(_mla_prep_kernel, qr=qr, kvr=kvr, rope=rope),
        grid=(m // tr,),
        in_specs=[pl.BlockSpec((tr, wc), lambda i: (i, cb)),
                  pl.BlockSpec((1, qr), lambda i: (0, 0)), pl.BlockSpec((1, kvr), lambda i: (0, 0)),
                  row(LANE), row(LANE), row(LANE)],
        out_specs=[row(qr), row(kvr), row(rope), row(kvr + LANE)],
        out_shape=[jax.ShapeDtypeStruct((m, qr), BF16), jax.ShapeDtypeStruct((m, kvr), F32),
                   jax.ShapeDtypeStruct((m, rope), F32), jax.ShapeDtypeStruct((m, kvr + LANE), BF16)],
        compiler_params=_params(("parallel",), 2 * tr * (wc * 4 + 3 * LANE * 4 + qr * 2 + kvr * 4 + (kvr + LANE) * 2)),
        name="mla_prep",
    )(gc, q_norm.reshape(1, qr), kv_norm.reshape(1, kvr), *tabs)


def _attn_tile(qs, k, v, m_i, l_i, acc, mask):
    s = lax.dot_general(qs, k, (((1,), (1,)), ((), ())), preferred_element_type=F32)
    if mask is not None:
        s = jnp.where(mask, s, NEG)
    m_new = jnp.maximum(m_i, jnp.max(s, axis=1, keepdims=True))
    alpha = jnp.exp(m_i - m_new)
    pexp = jnp.exp(s - m_new)
    l_new = alpha * l_i + jnp.sum(pexp, axis=1, keepdims=True)
    acc_new = alpha * acc + _dot(pexp.astype(BF16), v)
    return m_new, l_new, acc_new


def _q_scaled(q_ref, cos_ref, sina_ref, sinb_ref, *, nope, rope, scale):
    qf = q_ref[...].astype(F32)
    qrot = _rope_tile(qf[:, nope:], cos_ref, sina_ref, sinb_ref, rope // 2)
    return (jnp.concatenate([qf[:, :nope], qrot], axis=1) * scale).astype(BF16)


def _attn_prompt_kernel(q_ref, cos_ref, sina_ref, sinb_ref, k_ref, v_ref, g_ref, o_ref,
                        *, tq, nope, rope, dv, scale, n_pad, first):
    i = pl.program_id(1)
    qs = _q_scaled(q_ref, cos_ref, sina_ref, sinb_ref, nope=nope, rope=rope, scale=scale)

    def chunk_of(r):
        return jnp.where(r < first, 0, 1 + (r - first) // CHUNK)

    qrow = i * tq + lax.broadcasted_iota(jnp.int32, (tq, 1), 0)
    qchunk = chunk_of(qrow)

    def masked(j, carry):
        kcol = j * tq + lax.broadcasted_iota(jnp.int32, (1, tq), 1)
        mask = jnp.where(kcol >= n_pad, chunk_of(kcol), jnp.iinfo(jnp.int32).max) <= qchunk
        start = pl.multiple_of(j * tq, tq)
        return _attn_tile(qs, k_ref[pl.ds(start, tq), :], v_ref[pl.ds(start, tq), :], *carry, mask)

    def plain(j, carry):
        start = pl.multiple_of(j * tq, tq)
        return _attn_tile(qs, k_ref[pl.ds(start, tq), :], v_ref[pl.ds(start, tq), :], *carry, None)

    carry = (jnp.full((tq, 1), NEG, F32), jnp.zeros((tq, 1), F32), jnp.zeros((tq, dv), F32))
    carry = masked(0, carry)
    carry = lax.fori_loop(1, i, plain, carry)
    carry = masked(jnp.maximum(i, 1), carry)
    _, l_i, acc = carry
    o_ref[...] = (acc / l_i * _silu(g_ref[...].astype(F32))).astype(o_ref.dtype)


def _attn_sample_kernel(q_ref, cos_ref, sina_ref, sinb_ref, k_ref, v_ref, g_ref, prev_ref, o_ref,
                        *, nope, rope, dv, scale, n_valid):
    del prev_ref
    qs = _q_scaled(q_ref, cos_ref, sina_ref, sinb_ref, nope=nope, rope=rope, scale=scale)
    lq, lk = qs.shape[0], k_ref.shape[0]
    mask = lax.broadcasted_iota(jnp.int32, (1, lk), 1) < n_valid
    carry = (jnp.full((lq, 1), NEG, F32), jnp.zeros((lq, 1), F32), jnp.zeros((lq, dv), F32))
    _, l_i, acc = _attn_tile(qs, k_ref[...], v_ref[...], *carry, mask)
    o_ref[...] = (acc / l_i * _silu(g_ref[...].astype(F32))).astype(o_ref.dtype)


def _mla_layer(u, lay, w_in, q_norm, w_uq, kv_norm, w_ukv, w_out, cache_ckv, cache_kr, cache_meta_ckv,
               cache_meta_kr):
    m, d = u.shape
    lp, nb, ls, n_pad, first, n_meta = lay['lp'], lay['nb'], lay['ls'], lay['n_pad'], lay['first'], lay['n_meta']
    qr, = q_norm.shape
    kvr, = kv_norm.shape
    rope = cache_kr.shape[-1]
    width = w_out.shape[0]
    heads = (w_uq.shape[1] - w_ukv.shape[1] + width) // rope
    nope = w_uq.shape[1] // heads - rope
    dv = width // heads
    past = cache_ckv.shape[1]
    assert nope == LANE and dv == LANE and 2 * rope == LANE and (qr + kvr) % LANE == 0 and n_meta > 0
    dk = nope + LANE
    scale = float(nope + rope) ** -0.5

    wc = min(c for c in range(qr + kvr + LANE, width + 1, LANE) if width % c == 0)
    w_perm = jnp.concatenate([w_in[:, qr + kvr + rope:], w_in[:, :qr + kvr + rope],
                              jnp.zeros((d, wc - (qr + kvr + rope)), w_in.dtype)], axis=1)
    ncol = width + wc
    gc = _mm(u, w_perm, n_cols=ncol, out_dtype=F32, tn=_tile(math.gcd(width, wc), 512, LANE), name="mla_in")

    pos = np.concatenate([np.maximum(np.arange(lp) - n_pad, 0),
                          np.tile(n_meta + past + np.arange(ls), nb)]).astype(np.float32)
    half = rope // 2
    inv = ROPE_THETA ** (-jnp.arange(half, dtype=F32) / half)
    ang = jnp.asarray(pos)[:, None] * inv[None, :]
    cos, sin, zero = jnp.cos(ang), jnp.sin(ang), jnp.zeros((m, LANE - rope), F32)
    zh = jnp.zeros((m, half), F32)
    tabs = (jnp.concatenate([cos, cos, zero], axis=1), jnp.concatenate([-sin, zh, zero], axis=1),
            jnp.concatenate([zh, sin, zero], axis=1))

    cqn, ckv, kr, ckr = _mla_prep(gc, width, wc, q_norm, kv_norm, tabs, qr=qr, kvr=kvr, rope=rope)

    wq = jnp.pad(w_uq.reshape(qr, heads, nope + rope), ((0, 0), (0, 0), (0, LANE - rope))).reshape(qr, heads * dk)
    wkv = w_ukv.reshape(kvr, heads, nope + dv)
    eye = jnp.pad(jnp.eye(rope, dtype=w_ukv.dtype), ((0, LANE - rope), (0, LANE - rope)))
    wk = jnp.concatenate([
        jnp.pad(wkv[:, :, :nope], ((0, 0), (0, 0), (0, LANE))),
        jnp.pad(jnp.broadcast_to(eye[:, None, :], (LANE, heads, LANE)), ((0, 0), (0, 0), (nope, 0)))],
        axis=0).reshape(kvr + LANE, heads * dk)
    wv = jnp.pad(wkv[:, :, nope:], ((0, LANE), (0, 0), (0, 0))).reshape(kvr + LANE, heads * dv)

    q = _mm(cqn, wq, n_cols=heads * dk, out_dtype=BF16, name="mla_q")
    k_p = _mm(ckr, wk, n_cols=heads * dk, out_dtype=BF16, rows=lp, name="mla_k_prompt")
    v_p = _mm(ckr, wv, n_cols=heads * dv, out_dtype=BF16, rows=lp, name="mla_v_prompt")

    s_chunk = 1 + (past + np.arange(ls)) // CHUNK
    k_chunk = np.concatenate([1 + np.arange(past) // CHUNK, s_chunk, np.zeros(n_meta, np.int64)])
    assert (k_chunk[None, :] <= s_chunk[:, None]).all()
    n_valid = past + ls + n_meta
    lk = -(-n_valid // LANE) * LANE

    def cat_rows(ckv_part, kr_part):
        return jnp.concatenate([ckv_part.astype(BF16), kr_part.astype(BF16),
                                jnp.zeros(ckv_part.shape[:-1] + (LANE - rope,), BF16)], axis=-1)

    ckr_s = jnp.concatenate([
        cat_rows(cache_ckv, cache_kr), ckr[lp:].reshape(nb, ls, kvr + LANE),
        cat_rows(cache_meta_ckv, cache_meta_kr), jnp.zeros((nb, lk - n_valid, kvr + LANE), BF16)],
        axis=1).reshape(nb * lk, kvr + LANE)
    k_s = _mm(ckr_s, wk, n_cols=heads * dk, out_dtype=BF16, name="mla_k_sample")
    v_s = _mm(ckr_s, wv, n_cols=heads * dv, out_dtype=BF16, name="mla_v_sample")

    tq = _tile(lp, 384, LANE)
    assert first % CHUNK == 0 and tq % CHUNK == 0 and first <= tq and lp >= 2 * tq
    kw = dict(nope=nope, rope=rope, dv=dv, scale=scale)
    qspec = lambda rows, f: [pl.BlockSpec((rows, dk), f), pl.BlockSpec((rows, LANE), lambda *a: (f(*a)[0], 0)),
                             pl.BlockSpec((rows, LANE), lambda *a: (f(*a)[0], 0)),
                             pl.BlockSpec((rows, LANE), lambda *a: (f(*a)[0], 0))]
    o = pl.pallas_call(
        functools.partial(_attn_prompt_kernel, tq=tq, n_pad=n_pad, first=first, **kw),
        grid=(heads, lp // tq),
        in_specs=qspec(tq, lambda h, i: (i, h)) + [
            pl.BlockSpec((lp, dk), lambda h, i: (0, h)), pl.BlockSpec((lp, dv), lambda h, i: (0, h)),
            pl.BlockSpec((tq, dv), lambda h, i: (i, h))],
        out_specs=pl.BlockSpec((tq, dv), lambda h, i: (i, h)),
        out_shape=jax.ShapeDtypeStruct((m, width), BF16),
        compiler_params=_params(("parallel", "arbitrary"), 2 * lp * (dk + dv) * 2 + 12 * tq * tq * 4),
        name="mla_attn_prompt",
    )(q, *tabs, k_p, v_p, gc)
    assert lp % ls == 0
    rb0 = lp // ls
    o = pl.pallas_call(
        functools.partial(_attn_sample_kernel, n_valid=n_valid, **kw),
        grid=(nb, heads),
        in_specs=qspec(ls, lambda b, h: (rb0 + b, h)) + [
            pl.BlockSpec((lk, dk), lambda b, h: (b, h)), pl.BlockSpec((lk, dv), lambda b, h: (b, h)),
            pl.BlockSpec((ls, dv), lambda b, h: (rb0 + b, h)), pl.BlockSpec(memory_space=pl.ANY)],
        out_specs=pl.BlockSpec((ls, dv), lambda b, h: (rb0 + b, h)),
        out_shape=jax.ShapeDtypeStruct((m, width), BF16),
        input_output_aliases={7: 0},
        compiler_params=_params(("parallel", "arbitrary"), 2 * lk * (dk + dv) * 2 + 8 * ls * lk * 4),
        name="mla_attn_sample",
    )(q, *tabs, k_s, v_s, gc, o)
    out = _mm(o, w_out, n_cols=w_out.shape[1], out_dtype=F32, name="mla_out")
    return out, ckv, kr


def kernel(x_prompt, x_sample, state_ssd, state_conv, state_s5, cache_ckv, cache_kr, cache_meta_ckv, cache_meta_kr, meta_tokens, norm_pre, norm_post, ssd_w_in, ssd_conv_w, ssd_conv_b, ssd_dt_bias, ssd_a_log, ssd_d, ssd_norm, ssd_w_out, s5_w_in, s5_lam_re, s5_lam_im, s5_log_step, s5_b_re, s5_b_im, s5_c_re, s5_c_im, s5_d, s5_w_glu, s5_b_glu, s5_w_out, mla_w_in, mla_q_norm, mla_w_uq, mla_kv_norm, mla_w_ukv, mla_w_out):
    bp, seq, d = x_prompt.shape
    nb, ls, _ = x_sample.shape
    n_meta = meta_tokens.shape[0]
    depth = norm_pre.shape[0]
    assert bp == 1 and seq % LANE == 0 and ls == CHUNK
    n_pad = (-n_meta) % LANE
    first = n_pad + n_meta
    lp = first + seq
    m = lp + nb * ls
    lay = dict(lp=lp, nb=nb, ls=ls, n_pad=n_pad, first=first, n_meta=n_meta)

    h = jnp.concatenate([jnp.zeros((n_pad, d), F32), meta_tokens.astype(F32), x_prompt[0].astype(F32),
                         x_sample.reshape(nb * ls, d).astype(F32)], axis=0)
    u = _prenorm(h, norm_pre[0])

    ssd_p, ssd_s, conv_p, conv_s, s5_p, s5_s = [], [], [], [], [], []
    ckv_all, kr_all = [], []
    for i in range(depth):
        kind, j = i % 3, i // 3
        if kind == 0:
            o, sp, ss, cp, cs = _ssd_layer(u, lay, ssd_w_in[j], ssd_conv_w[j], ssd_conv_b[j], ssd_dt_bias[j],
                                           ssd_a_log[j], ssd_d[j], ssd_norm[j], ssd_w_out[j], state_conv[j],
                                           state_ssd[j])
            ssd_p.append(sp.astype(state_ssd.dtype))
            ssd_s.append(ss.astype(state_ssd.dtype))
            conv_p.append(cp)
            conv_s.append(cs)
        elif kind == 1:
            o, hpp, hss = _s5_layer(u, lay, s5_w_in[j], s5_lam_re[j], s5_lam_im[j], s5_log_step[j], s5_b_re[j],
                                    s5_b_im[j], s5_c_re[j], s5_c_im[j], s5_d[j], s5_w_glu[j], s5_b_glu[j],
                                    s5_w_out[j], state_s5[j])
            s5_p.append(hpp.astype(state_s5.dtype))
            s5_s.append(hss.astype(state_s5.dtype))
        else:
            o, ckv, kr = _mla_layer(u, lay, mla_w_in[j], mla_q_norm[j], mla_w_uq[j], mla_kv_norm[j],
                                    mla_w_ukv[j], mla_w_out[j], cache_ckv[j], cache_kr[j], cache_meta_ckv[j],
                                    cache_meta_kr[j])
            ckv_all.append(ckv)
            kr_all.append(kr)
        h, u = _postnorm(h, o, norm_post[i], norm_pre[(i + 1) % depth], n_pad)

    def split(a):
        return (a[:, None, n_pad:first], a[:, None, first:lp], a[:, lp:].reshape(a.shape[0], nb, ls, a.shape[-1]))

    mckv_p, ckv_p, ckv_s = split(jnp.stack(ckv_all))
    mkr_p, kr_p, kr_s = split(jnp.stack(kr_all))
    y_prompt = h[first:lp][None]
    y_sample = h[lp:].reshape(nb, ls, d)
    return (y_prompt, y_sample, jnp.stack(ssd_p), jnp.stack(ssd_s), jnp.stack(conv_p), jnp.stack(conv_s),
            jnp.stack(s5_p), jnp.stack(s5_s), mckv_p, mkr_p, ckv_p, kr_p, ckv_s, kr_s)
```

```python
import functools
import math

import numpy as np
import jax
import jax.numpy as jnp
from jax import lax
from jax.experimental import pallas as pl
from jax.experimental.pallas import tpu as pltpu

F32 = jnp.float32
BF16 = jnp.bfloat16

EPS = 1e-6
CHUNK = 64
ROPE_THETA = 10000.0

LANE = 128
SUBLANE = 8
VMEM_CAP = 56 * 1024 * 1024
NEG = -1e30
S5_T = 8


def _tile(dim, target, align=SUBLANE):
    best = None
    for t in range(align, min(dim, target) + 1, align):
        if dim % t == 0:
            best = t
    assert best is not None, (dim, target, align)
    return best


def _params(sem, vmem_bytes):
    limit = int(min(VMEM_CAP, max(vmem_bytes * 5 // 4 + (4 << 20), 16 << 20)))
    return pltpu.CompilerParams(dimension_semantics=sem, vmem_limit_bytes=limit)


def _sigmoid(x):
    return 1.0 / (1.0 + jnp.exp(-x))


def _silu(x):
    return x * _sigmoid(x)


def _softplus(x):
    return jnp.maximum(x, 0.0) + jnp.log1p(jnp.exp(-jnp.abs(x)))


def _gelu_tanh(x):
    return 0.5 * x * (1.0 + jnp.tanh(math.sqrt(2.0 / math.pi) * (x + 0.044715 * (x * x * x))))


def _split3(x):
    hi = x.astype(BF16)
    r1 = x - hi.astype(F32)
    mid = r1.astype(BF16)
    lo = (r1 - mid.astype(F32)).astype(BF16)
    return hi, mid, lo


def _dot(a, b):
    return jnp.dot(a, b, preferred_element_type=F32)


def _dot_f32_lhs(x, e):
    hi, mid, lo = _split3(x)
    return _dot(hi, e) + _dot(mid, e) + _dot(lo, e)


def _dot_f32_rhs(e, x):
    hi, mid, lo = _split3(x)
    return _dot(e, hi) + _dot(e, mid) + _dot(e, lo)


def _mm_kernel(*refs, nk, n_extra, epilogue):
    x_ref, w_ref = refs[0], refs[1]
    extra = refs[2:2 + n_extra]
    o_ref = refs[2 + n_extra]
    wb_ref = refs[3 + n_extra]
    i = pl.program_id(1)
    k = pl.program_id(2)

    @pl.when(i == 0)
    def _():
        wb_ref[k] = w_ref[...].astype(BF16)

    part = _dot(x_ref[...], wb_ref[k])

    def finish(acc):
        vals = [e[...] for e in extra]
        o_ref[...] = (epilogue(acc, *vals) if epilogue is not None else acc).astype(o_ref.dtype)

    if nk == 1:
        finish(part)
    else:
        acc_ref = refs[4 + n_extra]

        @pl.when(k == 0)
        def _():
            acc_ref[...] = part

        @pl.when(k > 0)
        def _():
            acc_ref[...] += part

        @pl.when(k == nk - 1)
        def _():
            finish(acc_ref[...])


def _mm(x, w, *, n_cols, col_off=0, out_dtype, rows=None, tn=512, tm_target=1024, tk_target=4096,
        epilogue=None, extras=(), name):
    m = x.shape[0] if rows is None else rows
    kdim = x.shape[1]
    assert w.shape[0] == kdim and x.dtype == BF16
    tn = min(tn, n_cols)
    assert n_cols % tn == 0 and col_off % tn == 0, (n_cols, col_off, tn)
    tm = _tile(m, tm_target, 16)
    tk = _tile(kdim, tk_target, LANE)
    nk = kdim // tk
    cb = col_off // tn
    grid = (n_cols // tn, m // tm, nk)

    def w_map(j, i, k):
        return (jnp.where(i == 0, k, nk - 1), j + cb)

    in_specs = [pl.BlockSpec((tm, tk), lambda j, i, k: (i, k)),
                pl.BlockSpec((tk, tn), w_map)]
    args = [x, w]
    for arr, kind, off in extras:
        ob = off // tn
        assert off % tn == 0
        if kind == 'tile':
            in_specs.append(pl.BlockSpec((tm, tn), lambda j, i, k, ob=ob: (i, j + ob)))
        else:
            in_specs.append(pl.BlockSpec((1, tn), lambda j, i, k, ob=ob: (0, j + ob)))
        args.append(arr)
    scratch = [pltpu.VMEM((nk, tk, tn), BF16)]
    if nk > 1:
        scratch.append(pltpu.VMEM((tm, tn), F32))
    osz = jnp.dtype(out_dtype).itemsize
    vmem = (2 * tm * tk * 2 + 2 * tk * tn * 4 + nk * tk * tn * 2 + tm * tn * 4 * 2 + 2 * tm * tn * osz
            + sum(2 * tm * tn * a.dtype.itemsize for a, kind, _ in extras if kind == 'tile'))
    return pl.pallas_call(
        functools.partial(_mm_kernel, nk=nk, n_extra=len(extras), epilogue=epilogue),
        grid=grid,
        in_specs=in_specs,
        out_specs=pl.BlockSpec((tm, tn), lambda j, i, k: (i, j)),
        out_shape=jax.ShapeDtypeStruct((m, n_cols), out_dtype),
        scratch_shapes=scratch,
        compiler_params=_params(("parallel", "arbitrary", "arbitrary"), vmem),
        name=name,
    )(*args)


def _rms(x, w):
    return x * lax.rsqrt(jnp.mean(x * x, axis=-1, keepdims=True) + EPS) * w


def _prenorm_kernel(h_ref, w_ref, u_ref):
    u_ref[...] = _rms(h_ref[...], w_ref[...]).astype(u_ref.dtype)


def _prenorm(h, w_pre):
    m, d = h.shape
    tr = _tile(m, 256)
    return pl.pallas_call(
        _prenorm_kernel,
        grid=(m // tr,),
        in_specs=[pl.BlockSpec((tr, d), lambda i: (i, 0)), pl.BlockSpec((1, d), lambda i: (0, 0))],
        out_specs=pl.BlockSpec((tr, d), lambda i: (i, 0)),
        out_shape=jax.ShapeDtypeStruct((m, d), BF16),
        compiler_params=_params(("parallel",), 2 * tr * d * 6),
        name="prenorm",
    )(h, w_pre.reshape(1, d))


def _postnorm_kernel(h_ref, o_ref, wpost_ref, wpre_ref, hout_ref, u_ref, *, tr, n_pad):
    rows = pl.program_id(0) * tr + lax.broadcasted_iota(jnp.int32, (tr, 1), 0)
    h = h_ref[...] + _rms(o_ref[...].astype(F32), wpost_ref[...])
    h = jnp.where(rows >= n_pad, h, 0.0)
    hout_ref[...] = h
    u_ref[...] = _rms(h, wpre_ref[...]).astype(u_ref.dtype)


def _postnorm(h, o, w_post, w_pre, n_pad):
    m, d = h.shape
    tr = _tile(m, 256)
    row = pl.BlockSpec((tr, d), lambda i: (i, 0))
    vec = pl.BlockSpec((1, d), lambda i: (0, 0))
    return pl.pallas_call(
        functools.partial(_postnorm_kernel, tr=tr, n_pad=n_pad),
        grid=(m // tr,),
        in_specs=[row, row, vec, vec],
        out_specs=[row, row],
        out_shape=[jax.ShapeDtypeStruct((m, d), F32), jax.ShapeDtypeStruct((m, d), BF16)],
        compiler_params=_params(("parallel",), 2 * tr * d * (4 + 4 + 4 + 2)),
        name="postnorm",
    )(h, o, w_post.reshape(1, d), w_pre.reshape(1, d))


def _ssd_kernel(z_ref, x_ref, b_ref, c_ref, cwx_ref, cwb_ref, cwc_ref, cbx_ref, cbb_ref, cbc_ref,
                bufx_ref, bufb_ref, bufc_ref, dtc_ref, dtr_ref, dtbc_ref, dtbr_ref, ac_ref, ar_ref,
                dsk_ref, nw_ref, h0_ref, o_ref, hout_ref, xext, bext, cext, st, ybuf,
                *, q, hg, p, n_pad, nchunks):
    c = pl.program_id(2)
    gw = hg * p

    @pl.when(c == 0)
    def _():
        xext[0:SUBLANE, :] = bufx_ref[...]
        bext[0:SUBLANE, :] = bufb_ref[...]
        cext[0:SUBLANE, :] = bufc_ref[...]
        st[...] = h0_ref[...].T

    def conv_silu(ext, raw_ref, w_ref, bias_ref):
        ext[SUBLANE:SUBLANE + q, :] = raw_ref[...]
        acc = bias_ref[...]
        for tap in range(4):
            acc = acc + w_ref[tap:tap + 1, :] * ext[SUBLANE - 3 + tap:SUBLANE - 3 + tap + q, :]
        ext[0:SUBLANE, :] = ext[q:q + SUBLANE, :]
        return _silu(acc)

    xs = conv_silu(xext, x_ref, cwx_ref, cbx_ref)
    bc = conv_silu(bext, b_ref, cwb_ref, cbb_ref)
    cc = conv_silu(cext, c_ref, cwc_ref, cbc_ref)

    rows_c = c * q + lax.broadcasted_iota(jnp.int32, (q, 1), 0)
    rows_r = c * q + lax.broadcasted_iota(jnp.int32, (1, q), 1)
    dt_c = jnp.where(rows_c >= n_pad, _softplus(dtc_ref[...] + dtbc_ref[...]), 0.0)
    dt_r = jnp.where(rows_r >= n_pad, _softplus(dtr_ref[...] + dtbr_ref[...]), 0.0)
    ii = lax.broadcasted_iota(jnp.int32, (q, q), 0)
    jj = lax.broadcasted_iota(jnp.int32, (q, q), 1)
    tri = jj <= ii
    tril = jnp.where(tri, 1.0, 0.0).astype(BF16)
    triu = jnp.where(ii <= jj, 1.0, 0.0).astype(BF16)
    acum = _dot_f32_rhs(tril, dt_c * ac_ref[...])
    acum_t = _dot_f32_lhs(dt_r * ar_ref[...], triu)

    eh = lax.broadcasted_iota(jnp.int32, (LANE, gw), 0)
    el = lax.broadcasted_iota(jnp.int32, (LANE, gw), 1)
    expand = jnp.where(el // p == eh, 1.0, 0.0).astype(BF16)
    dt_x = _dot_f32_lhs(dt_c, expand)
    acum_x = _dot_f32_lhs(acum, expand)
    last_x = acum_x[q - 1:q, :]
    decay_in = jnp.exp(acum_x)
    decay_out = jnp.exp(last_x - acum_x)
    decay_all = jnp.exp(last_x)

    xdt = xs * dt_x
    xdt_b = xdt.astype(BF16)
    xw_b = (xdt * decay_out).astype(BF16)
    bc_t = bc.T.astype(BF16)
    cc_b = cc.astype(BF16)
    cb = _dot(cc_b, bc_t)
    lane_head = lax.broadcasted_iota(jnp.int32, (q, LANE), 1) // p
    heads_per_slab = LANE // p

    for s in range(gw // LANE):
        sl = slice(s * LANE, (s + 1) * LANE)
        y = _dot(cc_b, st[:, sl].astype(BF16)) * decay_in[:, sl]
        xd = xdt_b[:, sl]
        for r in range(heads_per_slab):
            hh = s * heads_per_slab + r
            seg = acum[:, hh:hh + 1] - acum_t[hh:hh + 1, :]
            dec = jnp.exp(jnp.where(tri, seg, NEG))
            mh = (cb * dec).astype(BF16)
            y = y + _dot(mh, jnp.where(lane_head == r, xd, jnp.zeros_like(xd)))
        st[:, sl] = st[:, sl] * decay_all[:, sl] + _dot(bc_t, xw_b[:, sl])
        ybuf[:, sl] = y + xs[:, sl] * dsk_ref[:, sl]

    g = ybuf[...] * _silu(z_ref[...])
    g = g * lax.rsqrt(jnp.mean(g * g, axis=-1, keepdims=True) + EPS)
    o_ref[...] = (g * nw_ref[...]).astype(o_ref.dtype)

    @pl.when(c == nchunks - 1)
    def _():
        hout_ref[...] = st[...].T


def _ssd_scan(zx, dt_c, dt_r, prm, conv_buf8, h0, *, q, nseq, nchunks, row0, n_pad, out_prev, m, name):
    g_, hg, p, n, inner = prm['G'], prm['hg'], prm['P'], prm['N'], prm['inner']
    gw = hg * p
    hgp = dt_r.shape[2]
    assert row0 % q == 0
    rb0 = row0 // q
    xo, bo, co = inner // gw, (2 * inner) // n, (2 * inner + g_ * n) // n
    cbo, cco = inner // n, (inner + g_ * n) // n

    def rmap(s, g, c):
        return rb0 + s * nchunks + c

    in_specs = [
        pl.BlockSpec((q, gw), lambda s, g, c: (rmap(s, g, c), g)),
        pl.BlockSpec((q, gw), lambda s, g, c: (rmap(s, g, c), xo + g)),
        pl.BlockSpec((q, n), lambda s, g, c: (rmap(s, g, c), bo + g)),
        pl.BlockSpec((q, n), lambda s, g, c: (rmap(s, g, c), co + g)),
        pl.BlockSpec((4, gw), lambda s, g, c: (0, g)),
        pl.BlockSpec((4, n), lambda s, g, c: (0, cbo + g)),
        pl.BlockSpec((4, n), lambda s, g, c: (0, cco + g)),
        pl.BlockSpec((1, gw), lambda s, g, c: (0, g)),
        pl.BlockSpec((1, n), lambda s, g, c: (0, cbo + g)),
        pl.BlockSpec((1, n), lambda s, g, c: (0, cco + g)),
        pl.BlockSpec((None, SUBLANE, gw), lambda s, g, c: (s, 0, g)),
        pl.BlockSpec((None, SUBLANE, n), lambda s, g, c: (s, 0, cbo + g)),
        pl.BlockSpec((None, SUBLANE, n), lambda s, g, c: (s, 0, cco + g)),
        pl.BlockSpec((None, q, LANE), lambda s, g, c: (g, rmap(s, g, c), 0)),
        pl.BlockSpec((None, None, hgp, q), lambda s, g, c: (g, s * nchunks + c, 0, 0)),
        pl.BlockSpec((None, 1, LANE), lambda s, g, c: (g, 0, 0)),
        pl.BlockSpec((None, hgp, 1), lambda s, g, c: (g, 0, 0)),
        pl.BlockSpec((None, 1, LANE), lambda s, g, c: (g, 0, 0)),
        pl.BlockSpec((None, hgp, 1), lambda s, g, c: (g, 0, 0)),
        pl.BlockSpec((1, gw), lambda s, g, c: (0, g)),
        pl.BlockSpec((1, gw), lambda s, g, c: (0, g)),
        pl.BlockSpec((None, None, gw, n), lambda s, g, c: (s, g, 0, 0)),
    ]
    args = [zx, zx, zx, zx, prm['conv_w'], prm['conv_w'], prm['conv_w'], prm['conv_b'], prm['conv_b'],
            prm['conv_b'], conv_buf8, conv_buf8, conv_buf8, dt_c, dt_r, prm['dtb_c'], prm['dtb_r'],
            prm['a_c'], prm['a_r'], prm['d_x'], prm['norm_w'], h0]
    aliases = {}
    if out_prev is not None:
        in_specs.append(pl.BlockSpec(memory_space=pl.ANY))
        args.append(out_prev)
        aliases = {len(args) - 1: 0}

    def body(*refs):
        if out_prev is not None:
            refs = refs[:22] + refs[23:]
        _ssd_kernel(*refs, q=q, hg=hg, p=p, n_pad=n_pad, nchunks=nchunks)

    vmem = (2 * (2 * q * gw * 4 + 2 * q * n * 4 + q * gw * 2 + 2 * n * gw * 4) + (2 * q + 8) * gw * 4
            + n * gw * 4 + 24 * q * gw * 4)
    return pl.pallas_call(
        body,
        grid=(nseq, g_, nchunks),
        in_specs=in_specs,
        out_specs=[pl.BlockSpec((q, gw), lambda s, g, c: (rmap(s, g, c), g)),
                   pl.BlockSpec((None, None, gw, n), lambda s, g, c: (s, g, 0, 0))],
        out_shape=[jax.ShapeDtypeStruct((m, inner), BF16),
                   jax.ShapeDtypeStruct((nseq, g_, gw, n), F32)],
        scratch_shapes=[pltpu.VMEM((q + SUBLANE, gw), F32), pltpu.VMEM((q + SUBLANE, n), F32),
                        pltpu.VMEM((q + SUBLANE, n), F32), pltpu.VMEM((n, gw), F32),
                        pltpu.VMEM((q, gw), F32)],
        input_output_aliases=aliases,
        compiler_params=_params(("parallel", "parallel", "arbitrary"), vmem),
        name=name,
    )(*args)


def _ssd_layer(u, lay, w_in, conv_w, conv_b, dt_bias, a_log, d_skip, norm_w, w_out, conv_state, ssd_state):
    m = u.shape[0]
    lp, nb, ls, n_pad = lay['lp'], lay['nb'], lay['ls'], lay['n_pad']
    h_, = dt_bias.shape
    inner = w_out.shape[0]
    p = inner // h_
    n = ssd_state.shape[-1]
    conv_dim = conv_w.shape[1]
    g_ = (conv_dim - inner) // (2 * n)
    hg = h_ // g_
    gw = hg * p
    assert LANE % p == 0 and gw % LANE == 0 and n % LANE == 0 and hg <= LANE
    hgp = -(-hg // SUBLANE) * SUBLANE

    zx = _mm(u, w_in, n_cols=2 * inner + 2 * g_ * n, out_dtype=F32, tn=_tile(2 * inner + 2 * g_ * n, 512, LANE),
             name="ssd_in")
    w_dt = w_in[:, inner + conv_dim:].reshape(-1, g_, hg)
    w_dt = jnp.pad(w_dt, ((0, 0), (0, 0), (0, LANE - hg))).reshape(-1, g_ * LANE)
    dt_raw = _mm(u, w_dt, n_cols=g_ * LANE, out_dtype=F32, tn=LANE, name="ssd_dt")
    dt_c = dt_raw.reshape(m, g_, LANE).transpose(1, 0, 2)

    def dt_rows(q, start, stop):
        return dt_c[:, start:stop, :hgp].reshape(g_, (stop - start) // q, q, hgp).transpose(0, 1, 3, 2)

    a = -jnp.exp(a_log.astype(F32)).reshape(g_, hg)
    dtb = dt_bias.astype(F32).reshape(g_, hg)
    prm = dict(
        G=g_, hg=hg, P=p, N=n, inner=inner,
        conv_w=conv_w, conv_b=conv_b.reshape(1, conv_dim),
        dtb_c=jnp.pad(dtb, ((0, 0), (0, LANE - hg)))[:, None, :],
        dtb_r=jnp.pad(dtb, ((0, 0), (0, hgp - hg)))[:, :, None],
        a_c=jnp.pad(a, ((0, 0), (0, LANE - hg)))[:, None, :],
        a_r=jnp.pad(a, ((0, 0), (0, hgp - hg)))[:, :, None],
        d_x=jnp.repeat(d_skip.astype(F32), p).reshape(1, inner),
        norm_w=norm_w.reshape(1, inner),
    )

    def buf8(buf):
        return jnp.pad(buf.astype(F32), ((0, 0), (SUBLANE - 3, 0), (0, 0)))

    qp = 128
    y, hp = _ssd_scan(zx, dt_c, dt_rows(qp, 0, lp), prm, jnp.zeros((1, SUBLANE, conv_dim), F32),
                      jnp.zeros((1, g_, gw, n), F32), q=qp, nseq=1, nchunks=lp // qp, row0=0,
                      n_pad=n_pad, out_prev=None, m=m, name="ssd_scan_prompt")
    y, hs = _ssd_scan(zx, dt_c, dt_rows(ls, lp, m), prm, buf8(conv_state),
                      ssd_state.astype(F32).reshape(nb, g_, gw, n), q=ls, nseq=nb,
                      nchunks=1, row0=lp, n_pad=0, out_prev=y, m=m, name="ssd_scan_sample")
    o = _mm(y, w_out, n_cols=w_out.shape[1], out_dtype=F32, name="ssd_out")

    xbc = zx[:, inner:]
    conv_p = xbc[lp - 3:lp][None]
    conv_s = xbc[lp:].reshape(nb, ls, conv_dim)[:, ls - 3:]
    return o, hp.reshape(1, h_, p, n), hs.reshape(nb, h_, p, n), conv_p, conv_s


def _cmul(pr, pi, xr, xi):
    return pr * xr - pi * xi, pr * xi + pi * xr


def _s5_kernel(v_ref, um_ref, km_ref, cm_ref, p_ref, d_ref, h0_ref, y_ref, hout_ref, carry, yscr,
               *, rb, seg, nseg, sw):
    t_ = S5_T

    @pl.when(pl.program_id(1) == 0)
    def _():
        carry[...] = h0_ref[...]

    vs = [v_ref[pl.ds(i, rb, stride=t_), :] for i in range(t_)]
    vcat = jnp.concatenate(vs, axis=1).astype(BF16)
    x = _dot(vcat, um_ref[...])
    rowk = lax.broadcasted_iota(jnp.int32, (rb, 1), 0) % seg

    def shifted(a, s):
        return jnp.where(rowk >= s, pltpu.roll(a, s, axis=0), 0.0)

    row8 = lax.broadcasted_iota(jnp.int32, (rb, 1), 0) % SUBLANE
    s = 1
    while s < min(seg, SUBLANE):
        sh = jnp.where(row8 >= s, pltpu.roll(x, s, axis=0), 0.0)
        ar, ai = _cmul(p_ref[s:s + 1, :sw], p_ref[s:s + 1, sw:], sh[:, :sw], sh[:, sw:])
        x = x + jnp.concatenate([ar, ai], axis=1)
        s *= 2
    if seg > SUBLANE:
        p8r, p8i = p_ref[1:SUBLANE + 1, :sw], p_ref[1:SUBLANE + 1, sw:]
        groups = [x[0:SUBLANE, :]]
        for gi in range(1, seg // SUBLANE):
            prev = jnp.broadcast_to(groups[-1][SUBLANE - 1:SUBLANE, :], (SUBLANE, 2 * sw))
            ar, ai = _cmul(p8r, p8i, prev[:, :sw], prev[:, sw:])
            groups.append(x[gi * SUBLANE:(gi + 1) * SUBLANE, :] + jnp.concatenate([ar, ai], axis=1))
        x = jnp.concatenate(groups, axis=0)

    hin = carry[...]
    if nseg == 1:
        hin_rows = jnp.broadcast_to(hin, (rb, 2 * sw))
        ptab = p_ref[0:seg, :]
    else:
        hin_rows = jnp.broadcast_to(hin[:, None, :], (nseg, seg, 2 * sw)).reshape(rb, 2 * sw)
        ptab = jnp.concatenate([p_ref[0:seg, :]] * nseg, axis=0)
    cr, ci = _cmul(ptab[:, :sw], ptab[:, sw:], hin_rows[:, :sw], hin_rows[:, sw:])
    hb = shifted(x, 1) + jnp.concatenate([cr, ci], axis=1)

    y_all = _dot(vcat, km_ref[...]) + _dot(hb.astype(BF16), cm_ref[...])
    for j in range(t_):
        yj = y_all[:, j * LANE:(j + 1) * LANE] + vs[j] * d_ref[...]
        yscr[pl.ds(j, rb, stride=t_), :] = _gelu_tanh(yj)
    y_ref[...] = yscr[...].astype(y_ref.dtype)

    if nseg == 1:
        xlast = x[seg - 1:seg, :]
    else:
        pick = (lax.broadcasted_iota(jnp.int32, (nseg, rb), 1)
                == lax.broadcasted_iota(jnp.int32, (nseg, rb), 0) * seg + (seg - 1))
        xlast = _dot_f32_rhs(jnp.where(pick, 1.0, 0.0).astype(BF16), x)
    nr, ni = _cmul(p_ref[seg:seg + 1, :sw], p_ref[seg:seg + 1, sw:], hin[:, :sw], hin[:, sw:])
    new = xlast + jnp.concatenate([nr, ni], axis=1)
    carry[...] = new
    hout_ref[...] = new


def _s5_scan(vg, mats, ptab, d_row, h0, *, rb, seg, nseg, nblocks, m, name):
    um, km, cm = mats
    nt, tl, sw2 = um.shape
    sw = sw2 // 2
    width = nt * LANE
    rows = rb * S5_T
    assert seg % SUBLANE == 0 and (nseg == 1 or seg == SUBLANE) and rb == nseg * seg
    in_specs = [
        pl.BlockSpec((rows, LANE), lambda t, r: (r, t)),
        pl.BlockSpec((None, tl, sw2), lambda t, r: (t, 0, 0)),
        pl.BlockSpec((None, tl, tl), lambda t, r: (t, 0, 0)),
        pl.BlockSpec((None, sw2, tl), lambda t, r: (t, 0, 0)),
        pl.BlockSpec((None, ptab.shape[1], sw2), lambda t, r: (t, 0, 0)),
        pl.BlockSpec((1, LANE), lambda t, r: (0, t)),
        pl.BlockSpec((None, nseg, sw2), lambda t, r: (t, 0, 0)),
    ]
    vmem = 2 * (rows * LANE * 6 + tl * sw2 * 4 + tl * tl * 2 + ptab.shape[1] * sw2 * 4) + 12 * rb * sw2 * 4
    return pl.pallas_call(
        functools.partial(_s5_kernel, rb=rb, seg=seg, nseg=nseg, sw=sw),
        grid=(nt, nblocks),
        in_specs=in_specs,
        out_specs=[pl.BlockSpec((rows, LANE), lambda t, r: (r, t)),
                   pl.BlockSpec((None, nseg, sw2), lambda t, r: (t, 0, 0))],
        out_shape=[jax.ShapeDtypeStruct((m, width), BF16),
                   jax.ShapeDtypeStruct((nt, nseg, sw2), F32)],
        scratch_shapes=[pltpu.VMEM((nseg, sw2), F32), pltpu.VMEM((rows, LANE), F32)],
        compiler_params=_params(("parallel", "arbitrary"), vmem),
        name=name,
    )(vg, um, km, cm, ptab, d_row, h0)


def _s5_tables(lam_re, lam_im, log_step, b_re, b_im, c_re, c_im, max_seg):
    t_ = S5_T
    g_, p = lam_re.shape
    s_ = b_re.shape[-1]
    gl = LANE // s_
    nt = g_ // gl
    lam = lax.complex(lam_re.astype(F32), lam_im.astype(F32))
    delta = jnp.exp(log_step.astype(F32))[:, None]
    lam_bar = jnp.exp(lam * delta)
    b_bar = ((lam_bar - 1.0) / lam)[..., None] * lax.complex(b_re.astype(F32), b_im.astype(F32))
    c_t = lax.complex(c_re.astype(F32), c_im.astype(F32))

    def power(d):
        d = jnp.asarray(d, F32)[..., None, None]
        return jnp.exp((lam * delta) * d)

    same = jnp.arange(gl)[:, None] == jnp.arange(gl)[None, :]

    pw = power(jnp.arange(t_ + 1))
    ub = pw[t_ - 1 - jnp.arange(t_)][:, :, :, None] * b_bar[None]
    ub = ub.transpose(1, 0, 3, 2).reshape(nt, gl, t_, s_, p).transpose(0, 2, 1, 3, 4)
    ub = jnp.stack([ub.real, ub.imag], axis=4)[:, :, :, :, :, None, :]
    um = jnp.where(same[None, None, :, None, None, :, None], ub, 0.0).astype(BF16)
    um = um.reshape(nt, t_ * gl * s_, 2 * gl * p)

    kd = jnp.einsum('gop,dgp,gpi->dgio', c_t, pw[:t_], b_bar).real
    ii = jnp.arange(t_)[:, None]
    jj = jnp.arange(t_)[None, :]
    kf = jnp.where((jj >= ii)[:, :, None, None, None], kd[jnp.clip(jj - ii, 0, t_ - 1)], 0.0)
    kf = kf.transpose(2, 0, 3, 1, 4).reshape(nt, gl, t_, s_, t_, s_).transpose(0, 2, 1, 3, 4, 5)
    kf = kf[:, :, :, :, :, None, :]
    km = jnp.where(same[None, None, :, None, None, :, None], kf, 0.0).astype(BF16)
    km = km.reshape(nt, t_ * gl * s_, t_ * gl * s_)

    cw = c_t[None] * pw[1:][:, :, None, :]
    cw = cw.transpose(1, 3, 0, 2).reshape(nt, gl, p, t_, s_)
    cw = jnp.stack([cw.real, -cw.imag], axis=1)[:, :, :, :, :, None, :]
    cm = jnp.where(same[None, None, :, None, None, :, None], cw, 0.0).astype(BF16)
    cm = cm.reshape(nt, 2 * gl * p, t_ * gl * s_)

    nrow = -(-(max_seg + 1) // SUBLANE) * SUBLANE
    pt = power(t_ * jnp.arange(nrow))
    pt = pt.reshape(nrow, nt, gl * p).transpose(1, 0, 2)
    ptab = jnp.concatenate([pt.real, pt.imag], axis=-1)
    return (um, km, cm), ptab


def _s5_layer(u, lay, w_in, lam_re, lam_im, log_step, b_re, b_im, c_re, c_im, d_skip, w_glu, b_glu, w_out,
              state):
    m = u.shape[0]
    lp, nb, ls = lay['lp'], lay['nb'], lay['ls']
    width = w_glu.shape[0]
    g_, p = lam_re.shape
    s_ = b_re.shape[-1]
    gl = LANE // s_
    nt = g_ // gl
    sw = gl * p
    assert g_ * s_ == width and LANE % s_ == 0 and ls % S5_T == 0 and lp % S5_T == 0

    vg = _mm(u, w_in, n_cols=2 * width, out_dtype=F32, name="s5_in")
    rb_p = _tile(lp // S5_T, 384)
    seg_s = ls // S5_T
    mats, ptab = _s5_tables(lam_re, lam_im, log_step, b_re, b_im, c_re, c_im, max(rb_p, seg_s))
    d_row = d_skip.astype(F32).reshape(1, width)

    def to_tiles(st):
        b = st.shape[0]
        return st.astype(F32).reshape(b, nt, gl, p, 2).transpose(1, 0, 4, 2, 3).reshape(nt, b, 2 * sw)

    def from_tiles(h):
        b = h.shape[1]
        return h.reshape(nt, b, 2, gl, p).transpose(1, 0, 3, 4, 2).reshape(b, g_, p, 2)

    y, hp = _s5_scan(vg, mats, ptab, d_row, jnp.zeros((nt, 1, 2 * sw), F32), rb=rb_p, seg=rb_p, nseg=1,
                     nblocks=lp // (rb_p * S5_T), m=m, name="s5_scan_prompt")
    y_s, hs = _s5_scan(vg[lp:], mats, ptab, d_row, to_tiles(state), rb=nb * seg_s, seg=seg_s, nseg=nb,
                       nblocks=1, m=nb * ls, name="s5_scan_sample")
    y = lax.dynamic_update_slice(y, y_s, (lp, 0))

    def glu(acc, yv, gate, bias):
        return yv.astype(F32) * _sigmoid(acc + bias) * _silu(gate)

    t = _mm(y, w_glu, n_cols=width, out_dtype=BF16, epilogue=glu,
            extras=((y, 'tile', 0), (vg, 'tile', width), (b_glu.reshape(1, width), 'row', 0)), name="s5_glu")
    o = _mm(t, w_out, n_cols=w_out.shape[1], out_dtype=F32, name="s5_out")
    return o, from_tiles(hp), from_tiles(hs)


def _rope_tile(t, cos_ref, sina_ref, sinb_ref, half):
    return (t * cos_ref[...] + pltpu.roll(t, LANE - half, axis=1) * sina_ref[...]
            + pltpu.roll(t, half, axis=1) * sinb_ref[...])


def _mla_prep_kernel(c_ref, qw_ref, kvw_ref, cos_ref, sina_ref, sinb_ref,
                     cqn_ref, ckv_ref, kr_ref, ckr_ref, *, qr, kvr, rope):
    c = c_ref[...]
    cqn_ref[...] = _rms(c[:, :qr], qw_ref[...]).astype(cqn_ref.dtype)
    ckv = _rms(c[:, qr:qr + kvr], kvw_ref[...])
    ckv_ref[...] = ckv
    rot = _rope_tile(c[:, qr + kvr:qr + kvr + LANE], cos_ref, sina_ref, sinb_ref, rope // 2)
    kr_ref[...] = rot[:, :rope]
    ckr_ref[...] = jnp.concatenate([ckv, rot], axis=1).astype(ckr_ref.dtype)


def _mla_prep(gc, col_off, wc, q_norm, kv_norm, tabs, *, qr, kvr, rope):
    m = gc.shape[0]
    tr = _tile(m, 256)
    assert col_off % wc == 0 and wc >= qr + kvr + LANE
    cb = col_off // wc
    row = lambda w: pl.BlockSpec((tr, w), lambda i: (i, 0))
    return pl.pallas_call(
        functools.partial(_mla_prep_kernel, qr=qr, kvr=kvr, rope=rope),
        grid=(m // tr,),
        in_specs=[pl.BlockSpec((tr, wc), lambda i: (i, cb)),
                  pl.BlockSpec((1, qr), lambda i: (0, 0)), pl.BlockSpec((1, kvr), lambda i: (0, 0)),
                  row(LANE), row(LANE), row(LANE)],
        out_specs=[row(qr), row(kvr), row(rope), row(kvr + LANE)],
        out_shape=[jax.ShapeDtypeStruct((m, qr), BF16), jax.ShapeDtypeStruct((m, kvr), F32),
                   jax.ShapeDtypeStruct((m, rope), F32), jax.ShapeDtypeStruct((m, kvr + LANE), BF16)],
        compiler_params=_params(("parallel",), 2 * tr * (wc * 4 + 3 * LANE * 4 + qr * 2 + kvr * 4 + (kvr + LANE) * 2)),
        name="mla_prep",
    )(gc, q_norm.reshape(1, qr), kv_norm.reshape(1, kvr), *tabs)


def _attn_tile(qs, k, v, m_i, l_i, acc, mask):
    s = lax.dot_general(qs, k, (((1,), (1,)), ((), ())), preferred_element_type=F32)
    if mask is not None:
        s = jnp.where(mask, s, NEG)
    m_new = jnp.maximum(m_i, jnp.max(s, axis=1, keepdims=True))
    alpha = jnp.exp(m_i - m_new)
    pexp = jnp.exp(s - m_new)
    l_new = alpha * l_i + jnp.sum(pexp, axis=1, keepdims=True)
    acc_new = alpha * acc + _dot(pexp.astype(BF16), v)
    return m_new, l_new, acc_new


def _q_scaled(q_ref, cos_ref, sina_ref, sinb_ref, *, nope, rope, scale):
    qf = q_ref[...].astype(F32)
    qrot = _rope_tile(qf[:, nope:], cos_ref, sina_ref, sinb_ref, rope // 2)
    return (jnp.concatenate([qf[:, :nope], qrot], axis=1) * scale).astype(BF16)


def _attn_prompt_kernel(q_ref, cos_ref, sina_ref, sinb_ref, k_ref, v_ref, g_ref, o_ref,
                        q_scr, s_buf, p_buf, a_buf, m_scr, l_scr, acc_scr, bias_buf,
                        *, tq, nope, rope, dv, scale, n_pad, first):
    i = pl.program_id(1)
    q_scr[...] = _q_scaled(q_ref, cos_ref, sina_ref, sinb_ref, nope=nope, rope=rope,
                           scale=scale * math.log2(math.e))

    def chunk_of(r):
        return jnp.where(r < first, 0, 1 + (r - first) // CHUNK)

    qchunk = chunk_of(i * tq + lax.broadcasted_iota(jnp.int32, (tq, 1), 0))
    col = lax.broadcasted_iota(jnp.int32, (1, tq), 1)

    def bias_of(j):
        kcol = j * tq + col
        vis = jnp.where(kcol >= n_pad, chunk_of(kcol), jnp.iinfo(jnp.int32).max) <= qchunk
        return jnp.where(vis, 0.0, 2.0 * NEG)

    @pl.when(i == 0)
    def _():
        bias_buf[0] = jnp.zeros((tq, tq), F32)
        bias_buf[3] = jnp.full((tq, tq), 2.0 * NEG, F32)
        s_buf[1] = jnp.zeros((tq, tq), F32)

    @pl.when(i == 1)
    def _():
        bias_buf[1] = bias_of(0)

    bias_buf[2] = bias_of(i)
    p_buf[0] = jnp.zeros((tq, tq), BF16)
    a_buf[0] = jnp.ones((tq, LANE), F32)
    m_scr[...] = jnp.full((tq, LANE), NEG, F32)
    l_scr[...] = jnp.zeros((tq, LANE), F32)
    acc_scr[...] = jnp.zeros((tq, dv), F32)
    nl = tq // LANE

    def stage(t, sa, sb):
        ks = pl.multiple_of(jnp.minimum(t, i) * tq, tq)
        s_buf[sa] = lax.dot_general(q_scr[...], k_ref[pl.ds(ks, tq), :], (((1,), (1,)), ((), ())),
                                    preferred_element_type=F32)
        vs = pl.multiple_of(jnp.clip(t - 2, 0, i) * tq, tq)
        acc_scr[...] = a_buf[sa] * acc_scr[...] + _dot(p_buf[sa], v_ref[pl.ds(vs, tq), :])
        j = t - 1
        kind = jnp.where((j < 0) | (j > i), 3, jnp.where(j == i, 2, jnp.where(j == 0, 1, 0)))
        s = s_buf[sb] + bias_buf[kind]
        m_old = m_scr[...]
        m_new = jnp.maximum(m_old, jnp.max(s, axis=1, keepdims=True))
        a_new = jnp.exp2(m_old - m_new)
        pf = jnp.exp2(s - jnp.concatenate([m_new] * nl, axis=1))
        part = pf[:, 0:LANE]
        for c in range(1, nl):
            part = part + pf[:, c * LANE:(c + 1) * LANE]
        l_scr[...] = a_new * l_scr[...] + part
        m_scr[...] = m_new
        a_buf[sb] = a_new
        p_buf[sb] = pf.astype(BF16)

    def two_stages(tt, carry):
        stage(2 * tt, 0, 1)
        stage(2 * tt + 1, 1, 0)
        return carry

    lax.fori_loop(0, (i + 4) // 2, two_stages, 0)
    l_i = jnp.sum(l_scr[...], axis=1, keepdims=True)
    o_ref[...] = (acc_scr[...] / l_i * _silu(g_ref[...].astype(F32))).astype(o_ref.dtype)


def _attn_sample_kernel(q_ref, cos_ref, sina_ref, sinb_ref, k_ref, v_ref, g_ref, prev_ref, o_ref,
                        *, nope, rope, dv, scale, n_valid):
    del prev_ref
    qs = _q_scaled(q_ref, cos_ref, sina_ref, sinb_ref, nope=nope, rope=rope, scale=scale)
    lq, lk = qs.shape[0], k_ref.shape[0]
    mask = lax.broadcasted_iota(jnp.int32, (1, lk), 1) < n_valid
    carry = (jnp.full((lq, 1), NEG, F32), jnp.zeros((lq, 1), F32), jnp.zeros((lq, dv), F32))
    _, l_i, acc = _attn_tile(qs, k_ref[...], v_ref[...], *carry, mask)
    o_ref[...] = (acc / l_i * _silu(g_ref[...].astype(F32))).astype(o_ref.dtype)


def _mla_layer(u, lay, w_in, q_norm, w_uq, kv_norm, w_ukv, w_out, cache_ckv, cache_kr, cache_meta_ckv,
               cache_meta_kr):
    m, d = u.shape
    lp, nb, ls, n_pad, first, n_meta = lay['lp'], lay['nb'], lay['ls'], lay['n_pad'], lay['first'], lay['n_meta']
    qr, = q_norm.shape
    kvr, = kv_norm.shape
    rope = cache_kr.shape[-1]
    width = w_out.shape[0]
    heads = (w_uq.shape[1] - w_ukv.shape[1] + width) // rope
    nope = w_uq.shape[1] // heads - rope
    dv = width // heads
    past = cache_ckv.shape[1]
    assert nope == LANE and dv == LANE and 2 * rope == LANE and (qr + kvr) % LANE == 0 and n_meta > 0
    dk = nope + LANE
    scale = float(nope + rope) ** -0.5

    wc = min(c for c in range(qr + kvr + LANE, width + 1, LANE) if width % c == 0)
    w_perm = jnp.concatenate([w_in[:, qr + kvr + rope:], w_in[:, :qr + kvr + rope],
                              jnp.zeros((d, wc - (qr + kvr + rope)), w_in.dtype)], axis=1)
    ncol = width + wc
    gc = _mm(u, w_perm, n_cols=ncol, out_dtype=F32, tn=_tile(math.gcd(width, wc), 512, LANE), name="mla_in")

    pos = np.concatenate([np.maximum(np.arange(lp) - n_pad, 0),
                          np.tile(n_meta + past + np.arange(ls), nb)]).astype(np.float32)
    half = rope // 2
    inv = ROPE_THETA ** (-jnp.arange(half, dtype=F32) / half)
    ang = jnp.asarray(pos)[:, None] * inv[None, :]
    cos, sin, zero = jnp.cos(ang), jnp.sin(ang), jnp.zeros((m, LANE - rope), F32)
    zh = jnp.zeros((m, half), F32)
    tabs = (jnp.concatenate([cos, cos, zero], axis=1), jnp.concatenate([-sin, zh, zero], axis=1),
            jnp.concatenate([zh, sin, zero], axis=1))

    cqn, ckv, kr, ckr = _mla_prep(gc, width, wc, q_norm, kv_norm, tabs, qr=qr, kvr=kvr, rope=rope)

    wq = jnp.pad(w_uq.reshape(qr, heads, nope + rope), ((0, 0), (0, 0), (0, LANE - rope))).reshape(qr, heads * dk)
    wkv = w_ukv.reshape(kvr, heads, nope + dv)
    eye = jnp.pad(jnp.eye(rope, dtype=w_ukv.dtype), ((0, LANE - rope), (0, LANE - rope)))
    wk = jnp.concatenate([
        jnp.pad(wkv[:, :, :nope], ((0, 0), (0, 0), (0, LANE))),
        jnp.pad(jnp.broadcast_to(eye[:, None, :], (LANE, heads, LANE)), ((0, 0), (0, 0), (nope, 0)))],
        axis=0).reshape(kvr + LANE, heads * dk)
    wv = jnp.pad(wkv[:, :, nope:], ((0, LANE), (0, 0), (0, 0))).reshape(kvr + LANE, heads * dv)

    q = _mm(cqn, wq, n_cols=heads * dk, out_dtype=BF16, name="mla_q")
    k_p = _mm(ckr, wk, n_cols=heads * dk, out_dtype=BF16, rows=lp, name="mla_k_prompt")
    v_p = _mm(ckr, wv, n_cols=heads * dv, out_dtype=BF16, rows=lp, name="mla_v_prompt")

    s_chunk = 1 + (past + np.arange(ls)) // CHUNK
    k_chunk = np.concatenate([1 + np.arange(past) // CHUNK, s_chunk, np.zeros(n_meta, np.int64)])
    assert (k_chunk[None, :] <= s_chunk[:, None]).all()
    n_valid = past + ls + n_meta
    lk = -(-n_valid // LANE) * LANE

    def cat_rows(ckv_part, kr_part):
        return jnp.concatenate([ckv_part.astype(BF16), kr_part.astype(BF16),
                                jnp.zeros(ckv_part.shape[:-1] + (LANE - rope,), BF16)], axis=-1)

    ckr_s = jnp.concatenate([
        cat_rows(cache_ckv, cache_kr), ckr[lp:].reshape(nb, ls, kvr + LANE),
        cat_rows(cache_meta_ckv, cache_meta_kr), jnp.zeros((nb, lk - n_valid, kvr + LANE), BF16)],
        axis=1).reshape(nb * lk, kvr + LANE)
    k_s = _mm(ckr_s, wk, n_cols=heads * dk, out_dtype=BF16, name="mla_k_sample")
    v_s = _mm(ckr_s, wv, n_cols=heads * dv, out_dtype=BF16, name="mla_v_sample")

    tq = _tile(lp, 384, LANE)
    assert first % CHUNK == 0 and tq % CHUNK == 0 and first <= tq and lp >= 2 * tq
    kw = dict(nope=nope, rope=rope, dv=dv, scale=scale)
    qspec = lambda rows, f: [pl.BlockSpec((rows, dk), f), pl.BlockSpec((rows, LANE), lambda *a: (f(*a)[0], 0)),
                             pl.BlockSpec((rows, LANE), lambda *a: (f(*a)[0], 0)),
                             pl.BlockSpec((rows, LANE), lambda *a: (f(*a)[0], 0))]
    o = pl.pallas_call(
        functools.partial(_attn_prompt_kernel, tq=tq, n_pad=n_pad, first=first, **kw),
        grid=(heads, lp // tq),
        in_specs=qspec(tq, lambda h, i: (i, h)) + [
            pl.BlockSpec((lp, dk), lambda h, i: (0, h)), pl.BlockSpec((lp, dv), lambda h, i: (0, h)),
            pl.BlockSpec((tq, dv), lambda h, i: (i, h))],
        out_specs=pl.BlockSpec((tq, dv), lambda h, i: (i, h)),
        out_shape=jax.ShapeDtypeStruct((m, width), BF16),
        scratch_shapes=[pltpu.VMEM((tq, dk), BF16), pltpu.VMEM((2, tq, tq), F32), pltpu.VMEM((2, tq, tq), BF16),
                        pltpu.VMEM((2, tq, LANE), F32), pltpu.VMEM((tq, LANE), F32), pltpu.VMEM((tq, LANE), F32),
                        pltpu.VMEM((tq, dv), F32), pltpu.VMEM((4, tq, tq), F32)],
        compiler_params=_params(("parallel", "arbitrary"), 2 * lp * (dk + dv) * 2 + 16 * tq * tq * 4),
        name="mla_attn_prompt",
    )(q, *tabs, k_p, v_p, gc)
    assert lp % ls == 0
    rb0 = lp // ls
    o = pl.pallas_call(
        functools.partial(_attn_sample_kernel, n_valid=n_valid, **kw),
        grid=(nb, heads),
        in_specs=qspec(ls, lambda b, h: (rb0 + b, h)) + [
            pl.BlockSpec((lk, dk), lambda b, h: (b, h)), pl.BlockSpec((lk, dv), lambda b, h: (b, h)),
            pl.BlockSpec((ls, dv), lambda b, h: (rb0 + b, h)), pl.BlockSpec(memory_space=pl.ANY)],
        out_specs=pl.BlockSpec((ls, dv), lambda b, h: (rb0 + b, h)),
        out_shape=jax.ShapeDtypeStruct((m, width), BF16),
        input_output_aliases={7: 0},
        compiler_params=_params(("parallel", "arbitrary"), 2 * lk * (dk + dv) * 2 + 8 * ls * lk * 4),
        name="mla_attn_sample",
    )(q, *tabs, k_s, v_s, gc, o)
    out = _mm(o, w_out, n_cols=w_out.shape[1], out_dtype=F32, name="mla_out")
    return out, ckv, kr


def kernel(x_prompt, x_sample, state_ssd, state_conv, state_s5, cache_ckv, cache_kr, cache_meta_ckv, cache_meta_kr, meta_tokens, norm_pre, norm_post, ssd_w_in, ssd_conv_w, ssd_conv_b, ssd_dt_bias, ssd_a_log, ssd_d, ssd_norm, ssd_w_out, s5_w_in, s5_lam_re, s5_lam_im, s5_log_step, s5_b_re, s5_b_im, s5_c_re, s5_c_im, s5_d, s5_w_glu, s5_b_glu, s5_w_out, mla_w_in, mla_q_norm, mla_w_uq, mla_kv_norm, mla_w_ukv, mla_w_out):
    bp, seq, d = x_prompt.shape
    nb, ls, _ = x_sample.shape
    n_meta = meta_tokens.shape[0]
    depth = norm_pre.shape[0]
    assert bp == 1 and seq % LANE == 0 and ls == CHUNK
    n_pad = (-n_meta) % LANE
    first = n_pad + n_meta
    lp = first + seq
    m = lp + nb * ls
    lay = dict(lp=lp, nb=nb, ls=ls, n_pad=n_pad, first=first, n_meta=n_meta)

    h = jnp.concatenate([jnp.zeros((n_pad, d), F32), meta_tokens.astype(F32), x_prompt[0].astype(F32),
                         x_sample.reshape(nb * ls, d).astype(F32)], axis=0)
    u = _prenorm(h, norm_pre[0])

    ssd_p, ssd_s, conv_p, conv_s, s5_p, s5_s = [], [], [], [], [], []
    ckv_all, kr_all = [], []
    for i in range(depth):
        kind, j = i % 3, i // 3
        if kind == 0:
            o, sp, ss, cp, cs = _ssd_layer(u, lay, ssd_w_in[j], ssd_conv_w[j], ssd_conv_b[j], ssd_dt_bias[j],
                                           ssd_a_log[j], ssd_d[j], ssd_norm[j], ssd_w_out[j], state_conv[j],
                                           state_ssd[j])
            ssd_p.append(sp.astype(state_ssd.dtype))
            ssd_s.append(ss.astype(state_ssd.dtype))
            conv_p.append(cp)
            conv_s.append(cs)
        elif kind == 1:
            o, hpp, hss = _s5_layer(u, lay, s5_w_in[j], s5_lam_re[j], s5_lam_im[j], s5_log_step[j], s5_b_re[j],
                                    s5_b_im[j], s5_c_re[j], s5_c_im[j], s5_d[j], s5_w_glu[j], s5_b_glu[j],
                                    s5_w_out[j], state_s5[j])
            s5_p.append(hpp.astype(state_s5.dtype))
            s5_s.append(hss.astype(state_s5.dtype))
        else:
            o, ckv, kr = _mla_layer(u, lay, mla_w_in[j], mla_q_norm[j], mla_w_uq[j], mla_kv_norm[j],
                                    mla_w_ukv[j], mla_w_out[j], cache_ckv[j], cache_kr[j], cache_meta_ckv[j],
                                    cache_meta_kr[j])
            ckv_all.append(ckv)
            kr_all.append(kr)
        h, u = _postnorm(h, o, norm_post[i], norm_pre[(i + 1) % depth], n_pad)

    def split(a):
        return (a[:, None, n_pad:first], a[:, None, first:lp], a[:, lp:].reshape(a.shape[0], nb, ls, a.shape[-1]))

    mckv_p, ckv_p, ckv_s = split(jnp.stack(ckv_all))
    mkr_p, kr_p, kr_s = split(jnp.stack(kr_all))
    y_prompt = h[first:lp][None]
    y_sample = h[lp:].reshape(nb, ls, d)
    return (y_prompt, y_sample, jnp.stack(ssd_p), jnp.stack(ssd_s), jnp.stack(conv_p), jnp.stack(conv_s),
            jnp.stack(s5_p), jnp.stack(s5_s), mckv_p, mkr_p, ckv_p, kr_p, ckv_s, kr_s)
```

```python
import functools
import math

import numpy as np
import jax
import jax.numpy as jnp
from jax import lax
from jax.experimental import pallas as pl
from jax.experimental.pallas import tpu as pltpu

F32 = jnp.float32
BF16 = jnp.bfloat16

EPS = 1e-6
CHUNK = 64
ROPE_THETA = 10000.0

LANE = 128
SUBLANE = 8
VMEM_CAP = 56 * 1024 * 1024
W_BLOCK_BYTES = 8 * 1024 * 1024
NEG = -1e30
S5_T = 8


def _tile(dim, target, align=SUBLANE):
    best = None
    for t in range(align, min(dim, target) + 1, align):
        if dim % t == 0:
            best = t
    assert best is not None, (dim, target, align)
    return best


def _params(sem, vmem_bytes):
    limit = int(min(VMEM_CAP, max(vmem_bytes * 5 // 4 + (4 << 20), 16 << 20)))
    return pltpu.CompilerParams(dimension_semantics=sem, vmem_limit_bytes=limit)


def _sigmoid(x):
    return 0.5 + 0.5 * jnp.tanh(0.5 * x)


def _silu(x):
    h = 0.5 * x
    return h + h * jnp.tanh(h)


def _softplus(x):
    return jnp.maximum(x, 0.0) + jnp.log1p(jnp.exp(-jnp.abs(x)))


def _gelu_tanh(x):
    return 0.5 * x * (1.0 + jnp.tanh(math.sqrt(2.0 / math.pi) * (x + 0.044715 * (x * x * x))))


def _split3(x):
    hi = x.astype(BF16)
    r1 = x - hi.astype(F32)
    mid = r1.astype(BF16)
    lo = (r1 - mid.astype(F32)).astype(BF16)
    return hi, mid, lo


def _dot(a, b):
    return jnp.dot(a, b, preferred_element_type=F32)


def _dot_f32_lhs(x, e):
    hi, mid, lo = _split3(x)
    return _dot(hi, e) + _dot(mid, e) + _dot(lo, e)


def _dot_f32_rhs(e, x):
    hi, mid, lo = _split3(x)
    return _dot(e, hi) + _dot(e, mid) + _dot(e, lo)


def _mm_kernel(*refs, nk, n_extra, epilogue):
    x_ref, w_ref = refs[0], refs[1]
    extra = refs[2:2 + n_extra]
    o_ref = refs[2 + n_extra]
    wb_ref = refs[3 + n_extra]
    i = pl.program_id(1)
    k = pl.program_id(2)

    @pl.when(i == 0)
    def _():
        wb_ref[k] = w_ref[...].astype(BF16)

    part = _dot(x_ref[...], wb_ref[k])

    def finish(acc):
        vals = [e[...] for e in extra]
        o_ref[...] = (epilogue(acc, *vals) if epilogue is not None else acc).astype(o_ref.dtype)

    if nk == 1:
        finish(part)
    else:
        acc_ref = refs[4 + n_extra]

        @pl.when(k == 0)
        def _():
            acc_ref[...] = part

        @pl.when(k > 0)
        def _():
            acc_ref[...] += part

        @pl.when(k == nk - 1)
        def _():
            finish(acc_ref[...])


def _mm(x, w, *, n_cols, col_off=0, out_dtype, rows=None, tn=None, tm_target=1024, tk_target=4096,
        epilogue=None, extras=(), name):
    m = x.shape[0] if rows is None else rows
    kdim = x.shape[1]
    assert w.shape[0] == kdim and x.dtype == BF16
    tm = _tile(m, tm_target, 16)
    tk = _tile(kdim, tk_target, LANE)
    if tn is None:
        cap = max(LANE, min(2048, W_BLOCK_BYTES // (4 * tk)))
        tn = _tile(math.gcd(n_cols, col_off) if col_off else n_cols, cap, LANE)
    assert n_cols % tn == 0 and col_off % tn == 0, (n_cols, col_off, tn)
    nk = kdim // tk
    cb = col_off // tn
    grid = (n_cols // tn, m // tm, nk)

    def w_map(j, i, k):
        return (jnp.where(i == 0, k, nk - 1), j + cb)

    in_specs = [pl.BlockSpec((tm, tk), lambda j, i, k: (i, k)),
                pl.BlockSpec((tk, tn), w_map)]
    args = [x, w]
    for arr, kind, off in extras:
        ob = off // tn
        assert off % tn == 0
        if kind == 'tile':
            in_specs.append(pl.BlockSpec((tm, tn), lambda j, i, k, ob=ob: (i, j + ob)))
        else:
            in_specs.append(pl.BlockSpec((1, tn), lambda j, i, k, ob=ob: (0, j + ob)))
        args.append(arr)
    scratch = [pltpu.VMEM((nk, tk, tn), BF16)]
    if nk > 1:
        scratch.append(pltpu.VMEM((tm, tn), F32))
    osz = jnp.dtype(out_dtype).itemsize
    vmem = (2 * tm * tk * 2 + 2 * tk * tn * 4 + nk * tk * tn * 2 + tm * tn * 4 * 2 + 2 * tm * tn * osz
            + sum(2 * tm * tn * a.dtype.itemsize for a, kind, _ in extras if kind == 'tile'))
    return pl.pallas_call(
        functools.partial(_mm_kernel, nk=nk, n_extra=len(extras), epilogue=epilogue),
        grid=grid,
        in_specs=in_specs,
        out_specs=pl.BlockSpec((tm, tn), lambda j, i, k: (i, j)),
        out_shape=jax.ShapeDtypeStruct((m, n_cols), out_dtype),
        scratch_shapes=scratch,
        compiler_params=_params(("parallel", "arbitrary", "arbitrary"), vmem),
        name=name,
    )(*args)


def _rms(x, w):
    return x * lax.rsqrt(jnp.mean(x * x, axis=-1, keepdims=True) + EPS) * w


def _prenorm_kernel(h_ref, w_ref, u_ref):
    u_ref[...] = _rms(h_ref[...], w_ref[...]).astype(u_ref.dtype)


def _prenorm(h, w_pre):
    m, d = h.shape
    tr = _tile(m, 256)
    return pl.pallas_call(
        _prenorm_kernel,
        grid=(m // tr,),
        in_specs=[pl.BlockSpec((tr, d), lambda i: (i, 0)), pl.BlockSpec((1, d), lambda i: (0, 0))],
        out_specs=pl.BlockSpec((tr, d), lambda i: (i, 0)),
        out_shape=jax.ShapeDtypeStruct((m, d), BF16),
        compiler_params=_params(("parallel",), 2 * tr * d * 6),
        name="prenorm",
    )(h, w_pre.reshape(1, d))


def _postnorm_kernel(h_ref, o_ref, wpost_ref, wpre_ref, hout_ref, u_ref, *, tr, n_pad):
    rows = pl.program_id(0) * tr + lax.broadcasted_iota(jnp.int32, (tr, 1), 0)
    h = h_ref[...] + _rms(o_ref[...].astype(F32), wpost_ref[...])
    h = jnp.where(rows >= n_pad, h, 0.0)
    hout_ref[...] = h
    u_ref[...] = _rms(h, wpre_ref[...]).astype(u_ref.dtype)


def _postnorm(h, o, w_post, w_pre, n_pad):
    m, d = h.shape
    tr = _tile(m, 256)
    row = pl.BlockSpec((tr, d), lambda i: (i, 0))
    vec = pl.BlockSpec((1, d), lambda i: (0, 0))
    return pl.pallas_call(
        functools.partial(_postnorm_kernel, tr=tr, n_pad=n_pad),
        grid=(m // tr,),
        in_specs=[row, row, vec, vec],
        out_specs=[row, row],
        out_shape=[jax.ShapeDtypeStruct((m, d), F32), jax.ShapeDtypeStruct((m, d), BF16)],
        compiler_params=_params(("parallel",), 2 * tr * d * (4 + 4 + 4 + 2)),
        name="postnorm",
    )(h, o, w_post.reshape(1, d), w_pre.reshape(1, d))


def _ssd_kernel(z_ref, x_ref, b_ref, c_ref, cwx_ref, cwb_ref, cwc_ref, cbx_ref, cbb_ref, cbc_ref,
                bufx_ref, bufb_ref, bufc_ref, dtc_ref, dtr_ref, dtbc_ref, dtbr_ref, ac_ref, ar_ref,
                dsk_ref, nw_ref, h0_ref, o_ref, hout_ref, xext, bext, cext, st, ybuf,
                *, q, hg, p, n_pad, nchunks):
    c = pl.program_id(2)
    gw = hg * p

    @pl.when(c == 0)
    def _():
        xext[0:SUBLANE, :] = bufx_ref[...]
        bext[0:SUBLANE, :] = bufb_ref[...]
        cext[0:SUBLANE, :] = bufc_ref[...]
        st[...] = h0_ref[...].T

    def conv_silu(ext, raw_ref, w_ref, bias_ref):
        ext[SUBLANE:SUBLANE + q, :] = raw_ref[...]
        acc = bias_ref[...]
        for tap in range(4):
            acc = acc + w_ref[tap:tap + 1, :] * ext[SUBLANE - 3 + tap:SUBLANE - 3 + tap + q, :]
        ext[0:SUBLANE, :] = ext[q:q + SUBLANE, :]
        return _silu(acc)

    xs = conv_silu(xext, x_ref, cwx_ref, cbx_ref)
    bc = conv_silu(bext, b_ref, cwb_ref, cbb_ref)
    cc = conv_silu(cext, c_ref, cwc_ref, cbc_ref)

    rows_c = c * q + lax.broadcasted_iota(jnp.int32, (q, 1), 0)
    rows_r = c * q + lax.broadcasted_iota(jnp.int32, (1, q), 1)
    dt_c = jnp.where(rows_c >= n_pad, _softplus(dtc_ref[...] + dtbc_ref[...]), 0.0)
    dt_r = jnp.where(rows_r >= n_pad, _softplus(dtr_ref[...] + dtbr_ref[...]), 0.0)
    ii = lax.broadcasted_iota(jnp.int32, (q, q), 0)
    jj = lax.broadcasted_iota(jnp.int32, (q, q), 1)
    tri = jj <= ii
    tril = jnp.where(tri, 1.0, 0.0).astype(BF16)
    triu = jnp.where(ii <= jj, 1.0, 0.0).astype(BF16)
    acum = _dot_f32_rhs(tril, dt_c * ac_ref[...])
    acum_t = _dot_f32_lhs(dt_r * ar_ref[...], triu)

    eh = lax.broadcasted_iota(jnp.int32, (LANE, gw), 0)
    el = lax.broadcasted_iota(jnp.int32, (LANE, gw), 1)
    expand = jnp.where(el // p == eh, 1.0, 0.0).astype(BF16)
    dt_x = _dot_f32_lhs(dt_c, expand)
    acum_x = _dot_f32_lhs(acum, expand)
    last_x = acum_x[q - 1:q, :]
    decay_in = jnp.exp(acum_x)
    decay_out = jnp.exp(last_x - acum_x)
    decay_all = jnp.exp(last_x)

    xdt = xs * dt_x
    xdt_b = xdt.astype(BF16)
    xw_b = (xdt * decay_out).astype(BF16)
    bc_t = bc.T.astype(BF16)
    cc_b = cc.astype(BF16)
    cb = _dot(cc_b, bc_t)
    lane_head = lax.broadcasted_iota(jnp.int32, (q, LANE), 1) // p
    heads_per_slab = LANE // p

    for s in range(gw // LANE):
        sl = slice(s * LANE, (s + 1) * LANE)
        y = _dot(cc_b, st[:, sl].astype(BF16)) * decay_in[:, sl]
        xd = xdt_b[:, sl]
        for r in range(heads_per_slab):
            hh = s * heads_per_slab + r
            seg = acum[:, hh:hh + 1] - acum_t[hh:hh + 1, :]
            dec = jnp.exp(jnp.where(tri, seg, NEG))
            mh = (cb * dec).astype(BF16)
            y = y + _dot(mh, jnp.where(lane_head == r, xd, jnp.zeros_like(xd)))
        st[:, sl] = st[:, sl] * decay_all[:, sl] + _dot(bc_t, xw_b[:, sl])
        ybuf[:, sl] = y + xs[:, sl] * dsk_ref[:, sl]

    g = ybuf[...] * _silu(z_ref[...])
    g = g * lax.rsqrt(jnp.mean(g * g, axis=-1, keepdims=True) + EPS)
    o_ref[...] = (g * nw_ref[...]).astype(o_ref.dtype)

    @pl.when(c == nchunks - 1)
    def _():
        hout_ref[...] = st[...].T


def _ssd_scan(zx, dt_c, dt_r, prm, conv_buf8, h0, *, q, nseq, nchunks, row0, n_pad, out_prev, m, name):
    g_, hg, p, n, inner = prm['G'], prm['hg'], prm['P'], prm['N'], prm['inner']
    gw = hg * p
    hgp = dt_r.shape[2]
    assert row0 % q == 0
    rb0 = row0 // q
    xo, bo, co = inner // gw, (2 * inner) // n, (2 * inner + g_ * n) // n
    cbo, cco = inner // n, (inner + g_ * n) // n

    def rmap(s, g, c):
        return rb0 + s * nchunks + c

    in_specs = [
        pl.BlockSpec((q, gw), lambda s, g, c: (rmap(s, g, c), g)),
        pl.BlockSpec((q, gw), lambda s, g, c: (rmap(s, g, c), xo + g)),
        pl.BlockSpec((q, n), lambda s, g, c: (rmap(s, g, c), bo + g)),
        pl.BlockSpec((q, n), lambda s, g, c: (rmap(s, g, c), co + g)),
        pl.BlockSpec((4, gw), lambda s, g, c: (0, g)),
        pl.BlockSpec((4, n), lambda s, g, c: (0, cbo + g)),
        pl.BlockSpec((4, n), lambda s, g, c: (0, cco + g)),
        pl.BlockSpec((1, gw), lambda s, g, c: (0, g)),
        pl.BlockSpec((1, n), lambda s, g, c: (0, cbo + g)),
        pl.BlockSpec((1, n), lambda s, g, c: (0, cco + g)),
        pl.BlockSpec((None, SUBLANE, gw), lambda s, g, c: (s, 0, g)),
        pl.BlockSpec((None, SUBLANE, n), lambda s, g, c: (s, 0, cbo + g)),
        pl.BlockSpec((None, SUBLANE, n), lambda s, g, c: (s, 0, cco + g)),
        pl.BlockSpec((None, q, LANE), lambda s, g, c: (g, rmap(s, g, c), 0)),
        pl.BlockSpec((None, None, hgp, q), lambda s, g, c: (g, s * nchunks + c, 0, 0)),
        pl.BlockSpec((None, 1, LANE), lambda s, g, c: (g, 0, 0)),
        pl.BlockSpec((None, hgp, 1), lambda s, g, c: (g, 0, 0)),
        pl.BlockSpec((None, 1, LANE), lambda s, g, c: (g, 0, 0)),
        pl.BlockSpec((None, hgp, 1), lambda s, g, c: (g, 0, 0)),
        pl.BlockSpec((1, gw), lambda s, g, c: (0, g)),
        pl.BlockSpec((1, gw), lambda s, g, c: (0, g)),
        pl.BlockSpec((None, None, gw, n), lambda s, g, c: (s, g, 0, 0)),
    ]
    args = [zx, zx, zx, zx, prm['conv_w'], prm['conv_w'], prm['conv_w'], prm['conv_b'], prm['conv_b'],
            prm['conv_b'], conv_buf8, conv_buf8, conv_buf8, dt_c, dt_r, prm['dtb_c'], prm['dtb_r'],
            prm['a_c'], prm['a_r'], prm['d_x'], prm['norm_w'], h0]
    aliases = {}
    if out_prev is not None:
        in_specs.append(pl.BlockSpec(memory_space=pl.ANY))
        args.append(out_prev)
        aliases = {len(args) - 1: 0}

    def body(*refs):
        if out_prev is not None:
            refs = refs[:22] + refs[23:]
        _ssd_kernel(*refs, q=q, hg=hg, p=p, n_pad=n_pad, nchunks=nchunks)

    vmem = (2 * (2 * q * gw * 4 + 2 * q * n * 4 + q * gw * 2 + 2 * n * gw * 4) + (2 * q + 8) * gw * 4
            + n * gw * 4 + 24 * q * gw * 4)
    return pl.pallas_call(
        body,
        grid=(nseq, g_, nchunks),
        in_specs=in_specs,
        out_specs=[pl.BlockSpec((q, gw), lambda s, g, c: (rmap(s, g, c), g)),
                   pl.BlockSpec((None, None, gw, n), lambda s, g, c: (s, g, 0, 0))],
        out_shape=[jax.ShapeDtypeStruct((m, inner), BF16),
                   jax.ShapeDtypeStruct((nseq, g_, gw, n), F32)],
        scratch_shapes=[pltpu.VMEM((q + SUBLANE, gw), F32), pltpu.VMEM((q + SUBLANE, n), F32),
                        pltpu.VMEM((q + SUBLANE, n), F32), pltpu.VMEM((n, gw), F32),
                        pltpu.VMEM((q, gw), F32)],
        input_output_aliases=aliases,
        compiler_params=_params(("parallel", "parallel", "arbitrary"), vmem),
        name=name,
    )(*args)


def _ssd_layer(u, lay, w_in, conv_w, conv_b, dt_bias, a_log, d_skip, norm_w, w_out, conv_state, ssd_state):
    m = u.shape[0]
    lp, nb, ls, n_pad = lay['lp'], lay['nb'], lay['ls'], lay['n_pad']
    h_, = dt_bias.shape
    inner = w_out.shape[0]
    p = inner // h_
    n = ssd_state.shape[-1]
    conv_dim = conv_w.shape[1]
    g_ = (conv_dim - inner) // (2 * n)
    hg = h_ // g_
    gw = hg * p
    assert LANE % p == 0 and gw % LANE == 0 and n % LANE == 0 and hg <= LANE
    hgp = -(-hg // SUBLANE) * SUBLANE

    zx = _mm(u, w_in, n_cols=2 * inner + 2 * g_ * n, out_dtype=F32, tn=_tile(2 * inner + 2 * g_ * n, 512, LANE),
             name="ssd_in")
    dt_raw = _mm(u, w_in, n_cols=h_, col_off=inner + conv_dim, out_dtype=F32, tn=min(h_, LANE), name="ssd_dt")
    dt_c = jnp.pad(dt_raw.reshape(m, g_, hg).transpose(1, 0, 2), ((0, 0), (0, 0), (0, LANE - hg)))

    def dt_rows(q, start, stop):
        return dt_c[:, start:stop, :hgp].reshape(g_, (stop - start) // q, q, hgp).transpose(0, 1, 3, 2)

    a = -jnp.exp(a_log.astype(F32)).reshape(g_, hg)
    dtb = dt_bias.astype(F32).reshape(g_, hg)
    prm = dict(
        G=g_, hg=hg, P=p, N=n, inner=inner,
        conv_w=conv_w, conv_b=conv_b.reshape(1, conv_dim),
        dtb_c=jnp.pad(dtb, ((0, 0), (0, LANE - hg)))[:, None, :],
        dtb_r=jnp.pad(dtb, ((0, 0), (0, hgp - hg)))[:, :, None],
        a_c=jnp.pad(a, ((0, 0), (0, LANE - hg)))[:, None, :],
        a_r=jnp.pad(a, ((0, 0), (0, hgp - hg)))[:, :, None],
        d_x=jnp.repeat(d_skip.astype(F32), p).reshape(1, inner),
        norm_w=norm_w.reshape(1, inner),
    )

    def buf8(buf):
        return jnp.pad(buf.astype(F32), ((0, 0), (SUBLANE - 3, 0), (0, 0)))

    qp = 128
    y, hp = _ssd_scan(zx, dt_c, dt_rows(qp, 0, lp), prm, jnp.zeros((1, SUBLANE, conv_dim), F32),
                      jnp.zeros((1, g_, gw, n), F32), q=qp, nseq=1, nchunks=lp // qp, row0=0,
                      n_pad=n_pad, out_prev=None, m=m, name="ssd_scan_prompt")
    y, hs = _ssd_scan(zx, dt_c, dt_rows(ls, lp, m), prm, buf8(conv_state),
                      ssd_state.astype(F32).reshape(nb, g_, gw, n), q=ls, nseq=nb,
                      nchunks=1, row0=lp, n_pad=0, out_prev=y, m=m, name="ssd_scan_sample")
    o = _mm(y, w_out, n_cols=w_out.shape[1], out_dtype=F32, name="ssd_out")

    xbc = zx[:, inner:]
    conv_p = xbc[lp - 3:lp][None]
    conv_s = xbc[lp:].reshape(nb, ls, conv_dim)[:, ls - 3:]
    return o, hp.reshape(1, h_, p, n), hs.reshape(nb, h_, p, n), conv_p, conv_s


def _cmul(pr, pi, xr, xi):
    return pr * xr - pi * xi, pr * xi + pi * xr


def _s5_kernel(v_ref, um_ref, km_ref, cm_ref, p_ref, d_ref, h0_ref, y_ref, hout_ref, carry, yscr,
               *, rb, seg, nseg, sw):
    t_ = S5_T

    @pl.when(pl.program_id(1) == 0)
    def _():
        carry[...] = h0_ref[...]

    vs = [v_ref[pl.ds(i, rb, stride=t_), :] for i in range(t_)]
    vcat = jnp.concatenate(vs, axis=1).astype(BF16)
    x = _dot(vcat, um_ref[...])
    rowk = lax.broadcasted_iota(jnp.int32, (rb, 1), 0) % seg

    def shifted(a, s):
        return jnp.where(rowk >= s, pltpu.roll(a, s, axis=0), 0.0)

    row8 = lax.broadcasted_iota(jnp.int32, (rb, 1), 0) % SUBLANE
    s = 1
    while s < min(seg, SUBLANE):
        sh = jnp.where(row8 >= s, pltpu.roll(x, s, axis=0), 0.0)
        ar, ai = _cmul(p_ref[s:s + 1, :sw], p_ref[s:s + 1, sw:], sh[:, :sw], sh[:, sw:])
        x = x + jnp.concatenate([ar, ai], axis=1)
        s *= 2
    if seg > SUBLANE:
        p8r, p8i = p_ref[1:SUBLANE + 1, :sw], p_ref[1:SUBLANE + 1, sw:]
        groups = [x[0:SUBLANE, :]]
        for gi in range(1, seg // SUBLANE):
            prev = jnp.broadcast_to(groups[-1][SUBLANE - 1:SUBLANE, :], (SUBLANE, 2 * sw))
            ar, ai = _cmul(p8r, p8i, prev[:, :sw], prev[:, sw:])
            groups.append(x[gi * SUBLANE:(gi + 1) * SUBLANE, :] + jnp.concatenate([ar, ai], axis=1))
        x = jnp.concatenate(groups, axis=0)

    hin = carry[...]
    if nseg == 1:
        hin_rows = jnp.broadcast_to(hin, (rb, 2 * sw))
        ptab = p_ref[0:seg, :]
    else:
        hin_rows = jnp.broadcast_to(hin[:, None, :], (nseg, seg, 2 * sw)).reshape(rb, 2 * sw)
        ptab = jnp.concatenate([p_ref[0:seg, :]] * nseg, axis=0)
    cr, ci = _cmul(ptab[:, :sw], ptab[:, sw:], hin_rows[:, :sw], hin_rows[:, sw:])
    hb = shifted(x, 1) + jnp.concatenate([cr, ci], axis=1)

    y_all = _dot(vcat, km_ref[...]) + _dot(hb.astype(BF16), cm_ref[...])
    for j in range(t_):
        yj = y_all[:, j * LANE:(j + 1) * LANE] + vs[j] * d_ref[...]
        yscr[pl.ds(j, rb, stride=t_), :] = _gelu_tanh(yj)
    y_ref[...] = yscr[...].astype(y_ref.dtype)

    if nseg == 1:
        xlast = x[seg - 1:seg, :]
    else:
        pick = (lax.broadcasted_iota(jnp.int32, (nseg, rb), 1)
                == lax.broadcasted_iota(jnp.int32, (nseg, rb), 0) * seg + (seg - 1))
        xlast = _dot_f32_rhs(jnp.where(pick, 1.0, 0.0).astype(BF16), x)
    nr, ni = _cmul(p_ref[seg:seg + 1, :sw], p_ref[seg:seg + 1, sw:], hin[:, :sw], hin[:, sw:])
    new = xlast + jnp.concatenate([nr, ni], axis=1)
    carry[...] = new
    hout_ref[...] = new


def _s5_scan(vg, mats, ptab, d_row, h0, *, rb, seg, nseg, nblocks, m, name):
    um, km, cm = mats
    nt, tl, sw2 = um.shape
    sw = sw2 // 2
    width = nt * LANE
    rows = rb * S5_T
    assert seg % SUBLANE == 0 and (nseg == 1 or seg == SUBLANE) and rb == nseg * seg
    in_specs = [
        pl.BlockSpec((rows, LANE), lambda t, r: (r, t)),
        pl.BlockSpec((None, tl, sw2), lambda t, r: (t, 0, 0)),
        pl.BlockSpec((None, tl, tl), lambda t, r: (t, 0, 0)),
        pl.BlockSpec((None, sw2, tl), lambda t, r: (t, 0, 0)),
        pl.BlockSpec((None, ptab.shape[1], sw2), lambda t, r: (t, 0, 0)),
        pl.BlockSpec((1, LANE), lambda t, r: (0, t)),
        pl.BlockSpec((None, nseg, sw2), lambda t, r: (t, 0, 0)),
    ]
    vmem = 2 * (rows * LANE * 6 + tl * sw2 * 4 + tl * tl * 2 + ptab.shape[1] * sw2 * 4) + 12 * rb * sw2 * 4
    return pl.pallas_call(
        functools.partial(_s5_kernel, rb=rb, seg=seg, nseg=nseg, sw=sw),
        grid=(nt, nblocks),
        in_specs=in_specs,
        out_specs=[pl.BlockSpec((rows, LANE), lambda t, r: (r, t)),
                   pl.BlockSpec((None, nseg, sw2), lambda t, r: (t, 0, 0))],
        out_shape=[jax.ShapeDtypeStruct((m, width), BF16),
                   jax.ShapeDtypeStruct((nt, nseg, sw2), F32)],
        scratch_shapes=[pltpu.VMEM((nseg, sw2), F32), pltpu.VMEM((rows, LANE), F32)],
        compiler_params=_params(("parallel", "arbitrary"), vmem),
        name=name,
    )(vg, um, km, cm, ptab, d_row, h0)


def _s5_tables(lam_re, lam_im, log_step, b_re, b_im, c_re, c_im, max_seg):
    t_ = S5_T
    g_, p = lam_re.shape
    s_ = b_re.shape[-1]
    gl = LANE // s_
    nt = g_ // gl
    lam = lax.complex(lam_re.astype(F32), lam_im.astype(F32))
    delta = jnp.exp(log_step.astype(F32))[:, None]
    lam_bar = jnp.exp(lam * delta)
    b_bar = ((lam_bar - 1.0) / lam)[..., None] * lax.complex(b_re.astype(F32), b_im.astype(F32))
    c_t = lax.complex(c_re.astype(F32), c_im.astype(F32))

    def power(d):
        d = jnp.asarray(d, F32)[..., None, None]
        return jnp.exp((lam * delta) * d)

    same = jnp.arange(gl)[:, None] == jnp.arange(gl)[None, :]

    pw = power(jnp.arange(t_ + 1))
    ub = pw[t_ - 1 - jnp.arange(t_)][:, :, :, None] * b_bar[None]
    ub = ub.transpose(1, 0, 3, 2).reshape(nt, gl, t_, s_, p).transpose(0, 2, 1, 3, 4)
    ub = jnp.stack([ub.real, ub.imag], axis=4)[:, :, :, :, :, None, :]
    um = jnp.where(same[None, None, :, None, None, :, None], ub, 0.0).astype(BF16)
    um = um.reshape(nt, t_ * gl * s_, 2 * gl * p)

    kd = jnp.einsum('gop,dgp,gpi->dgio', c_t, pw[:t_], b_bar).real
    ii = jnp.arange(t_)[:, None]
    jj = jnp.arange(t_)[None, :]
    kf = jnp.where((jj >= ii)[:, :, None, None, None], kd[jnp.clip(jj - ii, 0, t_ - 1)], 0.0)
    kf = kf.transpose(2, 0, 3, 1, 4).reshape(nt, gl, t_, s_, t_, s_).transpose(0, 2, 1, 3, 4, 5)
    kf = kf[:, :, :, :, :, None, :]
    km = jnp.where(same[None, None, :, None, None, :, None], kf, 0.0).astype(BF16)
    km = km.reshape(nt, t_ * gl * s_, t_ * gl * s_)

    cw = c_t[None] * pw[1:][:, :, None, :]
    cw = cw.transpose(1, 3, 0, 2).reshape(nt, gl, p, t_, s_)
    cw = jnp.stack([cw.real, -cw.imag], axis=1)[:, :, :, :, :, None, :]
    cm = jnp.where(same[None, None, :, None, None, :, None], cw, 0.0).astype(BF16)
    cm = cm.reshape(nt, 2 * gl * p, t_ * gl * s_)

    nrow = -(-(max_seg + 1) // SUBLANE) * SUBLANE
    pt = power(t_ * jnp.arange(nrow))
    pt = pt.reshape(nrow, nt, gl * p).transpose(1, 0, 2)
    ptab = jnp.concatenate([pt.real, pt.imag], axis=-1)
    return (um, km, cm), ptab


def _s5_layer(u, lay, w_in, lam_re, lam_im, log_step, b_re, b_im, c_re, c_im, d_skip, w_glu, b_glu, w_out,
              state):
    m = u.shape[0]
    lp, nb, ls = lay['lp'], lay['nb'], lay['ls']
    width = w_glu.shape[0]
    g_, p = lam_re.shape
    s_ = b_re.shape[-1]
    gl = LANE // s_
    nt = g_ // gl
    sw = gl * p
    assert g_ * s_ == width and LANE % s_ == 0 and ls % S5_T == 0 and lp % S5_T == 0

    vg = _mm(u, w_in, n_cols=2 * width, out_dtype=F32, name="s5_in")
    rb_p = _tile(lp // S5_T, 384)
    seg_s = ls // S5_T
    mats, ptab = _s5_tables(lam_re, lam_im, log_step, b_re, b_im, c_re, c_im, max(rb_p, seg_s))
    d_row = d_skip.astype(F32).reshape(1, width)

    def to_tiles(st):
        b = st.shape[0]
        return st.astype(F32).reshape(b, nt, gl, p, 2).transpose(1, 0, 4, 2, 3).reshape(nt, b, 2 * sw)

    def from_tiles(h):
        b = h.shape[1]
        return h.reshape(nt, b, 2, gl, p).transpose(1, 0, 3, 4, 2).reshape(b, g_, p, 2)

    y, hp = _s5_scan(vg, mats, ptab, d_row, jnp.zeros((nt, 1, 2 * sw), F32), rb=rb_p, seg=rb_p, nseg=1,
                     nblocks=lp // (rb_p * S5_T), m=m, name="s5_scan_prompt")
    y_s, hs = _s5_scan(vg[lp:], mats, ptab, d_row, to_tiles(state), rb=nb * seg_s, seg=seg_s, nseg=nb,
                       nblocks=1, m=nb * ls, name="s5_scan_sample")
    y = lax.dynamic_update_slice(y, y_s, (lp, 0))

    def glu(acc, yv, gate, bias):
        return yv.astype(F32) * _sigmoid(acc + bias) * _silu(gate)

    t = _mm(y, w_glu, n_cols=width, out_dtype=BF16, epilogue=glu,
            extras=((y, 'tile', 0), (vg, 'tile', width), (b_glu.reshape(1, width), 'row', 0)), name="s5_glu")
    o = _mm(t, w_out, n_cols=w_out.shape[1], out_dtype=F32, name="s5_out")
    return o, from_tiles(hp), from_tiles(hs)


def _rope_tile(t, cos_ref, sina_ref, sinb_ref, half):
    return (t * cos_ref[...] + pltpu.roll(t, LANE - half, axis=1) * sina_ref[...]
            + pltpu.roll(t, half, axis=1) * sinb_ref[...])


def _mla_prep_kernel(c_ref, qw_ref, kvw_ref, cos_ref, sina_ref, sinb_ref,
                     cqn_ref, ckv_ref, kr_ref, ckr_ref, *, qr, kvr, rope):
    c = c_ref[...]
    cqn_ref[...] = _rms(c[:, :qr], qw_ref[...]).astype(cqn_ref.dtype)
    ckv = _rms(c[:, qr:qr + kvr], kvw_ref[...])
    ckv_ref[...] = ckv
    rot = _rope_tile(c[:, qr + kvr:qr + kvr + LANE], cos_ref, sina_ref, sinb_ref, rope // 2)
    kr_ref[...] = rot[:, :rope]
    ckr_ref[...] = jnp.concatenate([ckv, rot], axis=1).astype(ckr_ref.dtype)


def _mla_prep(gc, col_off, wc, q_norm, kv_norm, tabs, *, qr, kvr, rope):
    m = gc.shape[0]
    tr = _tile(m, 256)
    assert col_off % wc == 0 and wc >= qr + kvr + LANE
    cb = col_off // wc
    row = lambda w: pl.BlockSpec((tr, w), lambda i: (i, 0))
    return pl.pallas_call(
        functools.partial(_mla_prep_kernel, qr=qr, kvr=kvr, rope=rope),
        grid=(m // tr,),
        in_specs=[pl.BlockSpec((tr, wc), lambda i: (i, cb)),
                  pl.BlockSpec((1, qr), lambda i: (0, 0)), pl.BlockSpec((1, kvr), lambda i: (0, 0)),
                  row(LANE), row(LANE), row(LANE)],
        out_specs=[row(qr), row(kvr), row(rope), row(kvr + LANE)],
        out_shape=[jax.ShapeDtypeStruct((m, qr), BF16), jax.ShapeDtypeStruct((m, kvr), F32),
                   jax.ShapeDtypeStruct((m, rope), F32), jax.ShapeDtypeStruct((m, kvr + LANE), BF16)],
        compiler_params=_params(("parallel",), 2 * tr * (wc * 4 + 3 * LANE * 4 + qr * 2 + kvr * 4 + (kvr + LANE) * 2)),
        name="mla_prep",
    )(gc, q_norm.reshape(1, qr), kv_norm.reshape(1, kvr), *tabs)


def _attn_tile(qs, k, v, m_i, l_i, acc, mask):
    s = lax.dot_general(qs, k, (((1,), (1,)), ((), ())), preferred_element_type=F32)
    if mask is not None:
        s = jnp.where(mask, s, NEG)
    m_new = jnp.maximum(m_i, jnp.max(s, axis=1, keepdims=True))
    alpha = jnp.exp(m_i - m_new)
    pexp = jnp.exp(s - m_new)
    l_new = alpha * l_i + jnp.sum(pexp, axis=1, keepdims=True)
    acc_new = alpha * acc + _dot(pexp.astype(BF16), v)
    return m_new, l_new, acc_new


def _q_scaled(q_ref, cos_ref, sina_ref, sinb_ref, *, nope, rope, scale):
    qf = q_ref[...].astype(F32)
    qrot = _rope_tile(qf[:, nope:], cos_ref, sina_ref, sinb_ref, rope // 2)
    return (jnp.concatenate([qf[:, :nope], qrot], axis=1) * scale).astype(BF16)


def _attn_prompt_kernel(q_ref, cos_ref, sina_ref, sinb_ref, k_ref, v_ref, g_ref, o_ref,
                        q_scr, s_buf, p_buf, a_buf, m_scr, l_scr, acc_scr, bias_buf,
                        *, tq, tk, nope, rope, dv, scale, n_pad, first):
    i = pl.program_id(1)
    q_scr[...] = _q_scaled(q_ref, cos_ref, sina_ref, sinb_ref, nope=nope, rope=rope,
                           scale=scale * math.log2(math.e))

    def chunk_of(r):
        return jnp.where(r < first, 0, 1 + (r - first) // CHUNK)

    qchunk = chunk_of(i * tq + lax.broadcasted_iota(jnp.int32, (tq, 1), 0))
    col = lax.broadcasted_iota(jnp.int32, (1, tk), 1)
    ja = (i * tq) // tk
    jl = (i * tq + tq - 1) // tk

    def bias_of(j):
        kcol = j * tk + col
        vis = jnp.where(kcol >= n_pad, chunk_of(kcol), jnp.iinfo(jnp.int32).max) <= qchunk
        return jnp.where(vis, 0.0, 2.0 * NEG)

    @pl.when(i == 0)
    def _():
        bias_buf[0] = jnp.zeros((tq, tk), F32)
        bias_buf[4] = jnp.full((tq, tk), 2.0 * NEG, F32)
        s_buf[1] = jnp.zeros((tq, tk), F32)

    @pl.when(i == -(-tk // tq))
    def _():
        bias_buf[1] = bias_of(0)

    bias_buf[2] = bias_of(ja)
    bias_buf[3] = bias_of(jl)
    p_buf[0] = jnp.zeros((tq, tk), BF16)
    a_buf[0] = jnp.ones((tq, LANE), F32)
    m_scr[...] = jnp.full((tq, LANE), NEG, F32)
    l_scr[...] = jnp.zeros((tq, LANE), F32)
    acc_scr[...] = jnp.zeros((tq, dv), F32)
    nl = tk // LANE

    def stage(t, sa, sb):
        ks = pl.multiple_of(jnp.minimum(t, jl) * tk, tk)
        s_buf[sa] = lax.dot_general(q_scr[...], k_ref[pl.ds(ks, tk), :], (((1,), (1,)), ((), ())),
                                    preferred_element_type=F32)
        vs = pl.multiple_of(jnp.clip(t - 2, 0, jl) * tk, tk)
        acc_scr[...] = a_buf[sa] * acc_scr[...] + _dot(p_buf[sa], v_ref[pl.ds(vs, tk), :])
        j = t - 1
        kind = jnp.where((j < 0) | (j > jl), 4,
                         jnp.where(j == jl, 3, jnp.where(j == ja, 2, jnp.where(j == 0, 1, 0))))
        s = s_buf[sb] + bias_buf[kind]
        m_old = m_scr[...]
        m_new = jnp.maximum(m_old, jnp.max(s, axis=1, keepdims=True))
        a_new = jnp.exp2(m_old - m_new)
        pf = jnp.exp2(s - jnp.concatenate([m_new] * nl, axis=1))
        part = pf[:, 0:LANE]
        for c in range(1, nl):
            part = part + pf[:, c * LANE:(c + 1) * LANE]
        l_scr[...] = a_new * l_scr[...] + part
        m_scr[...] = m_new
        a_buf[sb] = a_new
        p_buf[sb] = pf.astype(BF16)

    def four_stages(tt, carry):
        stage(4 * tt, 0, 1)
        stage(4 * tt + 1, 1, 0)
        stage(4 * tt + 2, 0, 1)
        stage(4 * tt + 3, 1, 0)
        return carry

    lax.fori_loop(0, (jl + 6) // 4, four_stages, 0)
    l_i = jnp.sum(l_scr[...], axis=1, keepdims=True)
    o_ref[...] = (acc_scr[...] / l_i * _silu(g_ref[...].astype(F32))).astype(o_ref.dtype)


def _attn_sample_kernel(q_ref, cos_ref, sina_ref, sinb_ref, k_ref, v_ref, g_ref, prev_ref, o_ref,
                        *, nope, rope, dv, scale, n_valid):
    del prev_ref
    qs = _q_scaled(q_ref, cos_ref, sina_ref, sinb_ref, nope=nope, rope=rope, scale=scale)
    lq, lk = qs.shape[0], k_ref.shape[0]
    mask = lax.broadcasted_iota(jnp.int32, (1, lk), 1) < n_valid
    carry = (jnp.full((lq, 1), NEG, F32), jnp.zeros((lq, 1), F32), jnp.zeros((lq, dv), F32))
    _, l_i, acc = _attn_tile(qs, k_ref[...], v_ref[...], *carry, mask)
    o_ref[...] = (acc / l_i * _silu(g_ref[...].astype(F32))).astype(o_ref.dtype)


def _mla_layer(u, lay, w_in, q_norm, w_uq, kv_norm, w_ukv, w_out, cache_ckv, cache_kr, cache_meta_ckv,
               cache_meta_kr):
    m, d = u.shape
    lp, nb, ls, n_pad, first, n_meta = lay['lp'], lay['nb'], lay['ls'], lay['n_pad'], lay['first'], lay['n_meta']
    qr, = q_norm.shape
    kvr, = kv_norm.shape
    rope = cache_kr.shape[-1]
    width = w_out.shape[0]
    heads = (w_uq.shape[1] - w_ukv.shape[1] + width) // rope
    nope = w_uq.shape[1] // heads - rope
    dv = width // heads
    past = cache_ckv.shape[1]
    assert nope == LANE and dv == LANE and 2 * rope == LANE and (qr + kvr) % LANE == 0 and n_meta > 0
    dk = nope + LANE
    scale = float(nope + rope) ** -0.5

    wc = min(c for c in range(qr + kvr + LANE, width + 1, LANE) if width % c == 0)
    w_perm = jnp.concatenate([w_in[:, qr + kvr + rope:], w_in[:, :qr + kvr + rope],
                              jnp.zeros((d, wc - (qr + kvr + rope)), w_in.dtype)], axis=1)
    ncol = width + wc
    gc = _mm(u, w_perm, n_cols=ncol, out_dtype=F32, tn=_tile(math.gcd(width, wc), 512, LANE), name="mla_in")

    pos = np.concatenate([np.maximum(np.arange(lp) - n_pad, 0),
                          np.tile(n_meta + past + np.arange(ls), nb)]).astype(np.float32)
    half = rope // 2
    inv = ROPE_THETA ** (-jnp.arange(half, dtype=F32) / half)
    ang = jnp.asarray(pos)[:, None] * inv[None, :]
    cos, sin, zero = jnp.cos(ang), jnp.sin(ang), jnp.zeros((m, LANE - rope), F32)
    zh = jnp.zeros((m, half), F32)
    tabs = (jnp.concatenate([cos, cos, zero], axis=1), jnp.concatenate([-sin, zh, zero], axis=1),
            jnp.concatenate([zh, sin, zero], axis=1))

    cqn, ckv, kr, ckr = _mla_prep(gc, width, wc, q_norm, kv_norm, tabs, qr=qr, kvr=kvr, rope=rope)

    wq = jnp.pad(w_uq.reshape(qr, heads, nope + rope), ((0, 0), (0, 0), (0, LANE - rope))).reshape(qr, heads * dk)
    wkv = w_ukv.reshape(kvr, heads, nope + dv)
    eye = jnp.pad(jnp.eye(rope, dtype=w_ukv.dtype), ((0, LANE - rope), (0, LANE - rope)))
    wk = jnp.concatenate([
        jnp.pad(wkv[:, :, :nope], ((0, 0), (0, 0), (0, LANE))),
        jnp.pad(jnp.broadcast_to(eye[:, None, :], (LANE, heads, LANE)), ((0, 0), (0, 0), (nope, 0)))],
        axis=0).reshape(kvr + LANE, heads * dk)
    wv = jnp.pad(wkv[:, :, nope:], ((0, LANE), (0, 0), (0, 0))).reshape(kvr + LANE, heads * dv)

    tq = _tile(lp, 384, LANE)
    tk = 4 * LANE
    lkv = -(-lp // tk) * tk
    if not (tq <= tk and lkv <= m):
        tk, lkv = tq, lp
    q = _mm(cqn, wq, n_cols=heads * dk, out_dtype=BF16, name="mla_q")
    k_p = _mm(ckr, wk, n_cols=heads * dk, out_dtype=BF16, rows=lkv, name="mla_k_prompt")
    v_p = _mm(ckr, wv, n_cols=heads * dv, out_dtype=BF16, rows=lkv, name="mla_v_prompt")

    s_chunk = 1 + (past + np.arange(ls)) // CHUNK
    k_chunk = np.concatenate([1 + np.arange(past) // CHUNK, s_chunk, np.zeros(n_meta, np.int64)])
    assert (k_chunk[None, :] <= s_chunk[:, None]).all()
    n_valid = past + ls + n_meta
    lk = -(-n_valid // LANE) * LANE

    def cat_rows(ckv_part, kr_part):
        return jnp.concatenate([ckv_part.astype(BF16), kr_part.astype(BF16),
                                jnp.zeros(ckv_part.shape[:-1] + (LANE - rope,), BF16)], axis=-1)

    ckr_s = jnp.concatenate([
        cat_rows(cache_ckv, cache_kr), ckr[lp:].reshape(nb, ls, kvr + LANE),
        cat_rows(cache_meta_ckv, cache_meta_kr), jnp.zeros((nb, lk - n_valid, kvr + LANE), BF16)],
        axis=1).reshape(nb * lk, kvr + LANE)
    k_s = _mm(ckr_s, wk, n_cols=heads * dk, out_dtype=BF16, name="mla_k_sample")
    v_s = _mm(ckr_s, wv, n_cols=heads * dv, out_dtype=BF16, name="mla_v_sample")

    assert first % CHUNK == 0 and tq % CHUNK == 0 and first <= tq <= tk and lkv >= 2 * tk
    kw = dict(nope=nope, rope=rope, dv=dv, scale=scale)
    qspec = lambda rows, f: [pl.BlockSpec((rows, dk), f), pl.BlockSpec((rows, LANE), lambda *a: (f(*a)[0], 0)),
                             pl.BlockSpec((rows, LANE), lambda *a: (f(*a)[0], 0)),
                             pl.BlockSpec((rows, LANE), lambda *a: (f(*a)[0], 0))]
    o = pl.pallas_call(
        functools.partial(_attn_prompt_kernel, tq=tq, tk=tk, n_pad=n_pad, first=first, **kw),
        grid=(heads, lp // tq),
        in_specs=qspec(tq, lambda h, i: (i, h)) + [
            pl.BlockSpec((lkv, dk), lambda h, i: (0, h)), pl.BlockSpec((lkv, dv), lambda h, i: (0, h)),
            pl.BlockSpec((tq, dv), lambda h, i: (i, h))],
        out_specs=pl.BlockSpec((tq, dv), lambda h, i: (i, h)),
        out_shape=jax.ShapeDtypeStruct((m, width), BF16),
        scratch_shapes=[pltpu.VMEM((tq, dk), BF16), pltpu.VMEM((2, tq, tk), F32), pltpu.VMEM((2, tq, tk), BF16),
                        pltpu.VMEM((2, tq, LANE), F32), pltpu.VMEM((tq, LANE), F32), pltpu.VMEM((tq, LANE), F32),
                        pltpu.VMEM((tq, dv), F32), pltpu.VMEM((5, tq, tk), F32)],
        compiler_params=_params(("parallel", "arbitrary"), 2 * lkv * (dk + dv) * 2 + 20 * tq * tk * 4),
        name="mla_attn_prompt",
    )(q, *tabs, k_p, v_p, gc)
    assert lp % ls == 0
    rb0 = lp // ls
    o = pl.pallas_call(
        functools.partial(_attn_sample_kernel, n_valid=n_valid, **kw),
        grid=(nb, heads),
        in_specs=qspec(ls, lambda b, h: (rb0 + b, h)) + [
            pl.BlockSpec((lk, dk), lambda b, h: (b, h)), pl.BlockSpec((lk, dv), lambda b, h: (b, h)),
            pl.BlockSpec((ls, dv), lambda b, h: (rb0 + b, h)), pl.BlockSpec(memory_space=pl.ANY)],
        out_specs=pl.BlockSpec((ls, dv), lambda b, h: (rb0 + b, h)),
        out_shape=jax.ShapeDtypeStruct((m, width), BF16),
        input_output_aliases={7: 0},
        compiler_params=_params(("parallel", "arbitrary"), 2 * lk * (dk + dv) * 2 + 8 * ls * lk * 4),
        name="mla_attn_sample",
    )(q, *tabs, k_s, v_s, gc, o)
    out = _mm(o, w_out, n_cols=w_out.shape[1], out_dtype=F32, name="mla_out")
    return out, ckv, kr


def kernel(x_prompt, x_sample, state_ssd, state_conv, state_s5, cache_ckv, cache_kr, cache_meta_ckv, cache_meta_kr, meta_tokens, norm_pre, norm_post, ssd_w_in, ssd_conv_w, ssd_conv_b, ssd_dt_bias, ssd_a_log, ssd_d, ssd_norm, ssd_w_out, s5_w_in, s5_lam_re, s5_lam_im, s5_log_step, s5_b_re, s5_b_im, s5_c_re, s5_c_im, s5_d, s5_w_glu, s5_b_glu, s5_w_out, mla_w_in, mla_q_norm, mla_w_uq, mla_kv_norm, mla_w_ukv, mla_w_out):
    bp, seq, d = x_prompt.shape
    nb, ls, _ = x_sample.shape
    n_meta = meta_tokens.shape[0]
    depth = norm_pre.shape[0]
    assert bp == 1 and seq % LANE == 0 and ls == CHUNK
    n_pad = (-n_meta) % LANE
    first = n_pad + n_meta
    lp = first + seq
    m = lp + nb * ls
    lay = dict(lp=lp, nb=nb, ls=ls, n_pad=n_pad, first=first, n_meta=n_meta)

    h = jnp.concatenate([jnp.zeros((n_pad, d), F32), meta_tokens.astype(F32), x_prompt[0].astype(F32),
                         x_sample.reshape(nb * ls, d).astype(F32)], axis=0)
    u = _prenorm(h, norm_pre[0])

    ssd_p, ssd_s, conv_p, conv_s, s5_p, s5_s = [], [], [], [], [], []
    ckv_all, kr_all = [], []
    for i in range(depth):
        kind, j = i % 3, i // 3
        if kind == 0:
            o, sp, ss, cp, cs = _ssd_layer(u, lay, ssd_w_in[j], ssd_conv_w[j], ssd_conv_b[j], ssd_dt_bias[j],
                                           ssd_a_log[j], ssd_d[j], ssd_norm[j], ssd_w_out[j], state_conv[j],
                                           state_ssd[j])
            ssd_p.append(sp.astype(state_ssd.dtype))
            ssd_s.append(ss.astype(state_ssd.dtype))
            conv_p.append(cp)
            conv_s.append(cs)
        elif kind == 1:
            o, hpp, hss = _s5_layer(u, lay, s5_w_in[j], s5_lam_re[j], s5_lam_im[j], s5_log_step[j], s5_b_re[j],
                                    s5_b_im[j], s5_c_re[j], s5_c_im[j], s5_d[j], s5_w_glu[j], s5_b_glu[j],
                                    s5_w_out[j], state_s5[j])
            s5_p.append(hpp.astype(state_s5.dtype))
            s5_s.append(hss.astype(state_s5.dtype))
        else:
            o, ckv, kr = _mla_layer(u, lay, mla_w_in[j], mla_q_norm[j], mla_w_uq[j], mla_kv_norm[j],
                                    mla_w_ukv[j], mla_w_out[j], cache_ckv[j], cache_kr[j], cache_meta_ckv[j],
                                    cache_meta_kr[j])
            ckv_all.append(ckv)
            kr_all.append(kr)
        h, u = _postnorm(h, o, norm_post[i], norm_pre[(i + 1) % depth], n_pad)

    def split(a):
        return (a[:, None, n_pad:first], a[:, None, first:lp], a[:, lp:].reshape(a.shape[0], nb, ls, a.shape[-1]))

    mckv_p, ckv_p, ckv_s = split(jnp.stack(ckv_all))
    mkr_p, kr_p, kr_s = split(jnp.stack(kr_all))
    y_prompt = h[first:lp][None]
    y_sample = h[lp:].reshape(nb, ls, d)
    return (y_prompt, y_sample, jnp.stack(ssd_p), jnp.stack(ssd_s), jnp.stack(conv_p), jnp.stack(conv_s),
            jnp.stack(s5_p), jnp.stack(s5_s), mckv_p, mkr_p, ckv_p, kr_p, ckv_s, kr_s)
```

```python
import functools
import math

import numpy as np
import jax
import jax.numpy as jnp
from jax import lax
from jax.experimental import pallas as pl
from jax.experimental.pallas import tpu as pltpu

F32 = jnp.float32
BF16 = jnp.bfloat16

EPS = 1e-6
CHUNK = 64
ROPE_THETA = 10000.0

LANE = 128
SUBLANE = 8
VMEM_CAP = 56 * 1024 * 1024
W_BLOCK_BYTES = 8 * 1024 * 1024
NEG = -1e30
S5_T = 8


def _tile(dim, target, align=SUBLANE):
    best = None
    for t in range(align, min(dim, target) + 1, align):
        if dim % t == 0:
            best = t
    assert best is not None, (dim, target, align)
    return best


def _params(sem, vmem_bytes):
    limit = int(min(VMEM_CAP, max(vmem_bytes * 5 // 4 + (4 << 20), 16 << 20)))
    return pltpu.CompilerParams(dimension_semantics=sem, vmem_limit_bytes=limit)


def _sigmoid(x):
    return 0.5 + 0.5 * jnp.tanh(0.5 * x)


def _silu(x):
    h = 0.5 * x
    return h + h * jnp.tanh(h)


def _softplus(x):
    return jnp.maximum(x, 0.0) + jnp.log1p(jnp.exp(-jnp.abs(x)))


def _gelu_tanh(x):
    return 0.5 * x * (1.0 + jnp.tanh(math.sqrt(2.0 / math.pi) * (x + 0.044715 * (x * x * x))))


def _split3(x):
    hi = x.astype(BF16)
    r1 = x - hi.astype(F32)
    mid = r1.astype(BF16)
    lo = (r1 - mid.astype(F32)).astype(BF16)
    return hi, mid, lo


def _dot(a, b):
    return jnp.dot(a, b, preferred_element_type=F32)


def _dot_f32_lhs(x, e):
    hi, mid, lo = _split3(x)
    return _dot(hi, e) + _dot(mid, e) + _dot(lo, e)


def _dot_f32_rhs(e, x):
    hi, mid, lo = _split3(x)
    return _dot(e, hi) + _dot(e, mid) + _dot(e, lo)


def _mm_kernel(*refs, nk, n_extra, epilogue):
    x_ref, w_ref = refs[0], refs[1]
    extra = refs[2:2 + n_extra]
    o_ref = refs[2 + n_extra]
    wb_ref = refs[3 + n_extra]
    i = pl.program_id(1)
    k = pl.program_id(2)

    @pl.when(i == 0)
    def _():
        wb_ref[k] = w_ref[...].astype(BF16)

    part = _dot(x_ref[...], wb_ref[k])

    def finish(acc):
        vals = [e[...] for e in extra]
        o_ref[...] = (epilogue(acc, *vals) if epilogue is not None else acc).astype(o_ref.dtype)

    if nk == 1:
        finish(part)
    else:
        acc_ref = refs[4 + n_extra]

        @pl.when(k == 0)
        def _():
            acc_ref[...] = part

        @pl.when(k > 0)
        def _():
            acc_ref[...] += part

        @pl.when(k == nk - 1)
        def _():
            finish(acc_ref[...])


def _mm(x, w, *, n_cols, col_off=0, out_dtype, rows=None, tn=None, tm_target=1024, tk_target=4096,
        epilogue=None, extras=(), layer=None, name):
    m = x.shape[0] if rows is None else rows
    kdim = x.shape[1]
    assert w.shape[-2] == kdim and x.dtype == BF16 and w.ndim == (2 if layer is None else 3)
    tm = _tile(m, tm_target, 16)
    tk = _tile(kdim, tk_target, LANE)
    if tn is None:
        cap = max(LANE, min(2048, W_BLOCK_BYTES // (4 * tk)))
        tn = _tile(math.gcd(n_cols, col_off) if col_off else n_cols, cap, LANE)
    assert n_cols % tn == 0 and col_off % tn == 0, (n_cols, col_off, tn)
    nk = kdim // tk
    cb = col_off // tn
    grid = (n_cols // tn, m // tm, nk)

    def w_map(j, i, k):
        return (jnp.where(i == 0, k, nk - 1), j + cb)

    if layer is None:
        w_spec = pl.BlockSpec((tk, tn), w_map)
    else:
        w_spec = pl.BlockSpec((None, tk, tn), lambda j, i, k: (layer,) + w_map(j, i, k))
    in_specs = [pl.BlockSpec((tm, tk), lambda j, i, k: (i, k)), w_spec]
    args = [x, w]
    for arr, kind, off in extras:
        ob = off // tn
        assert off % tn == 0
        if kind == 'tile':
            in_specs.append(pl.BlockSpec((tm, tn), lambda j, i, k, ob=ob: (i, j + ob)))
        else:
            in_specs.append(pl.BlockSpec((1, tn), lambda j, i, k, ob=ob: (0, j + ob)))
        args.append(arr)
    scratch = [pltpu.VMEM((nk, tk, tn), BF16)]
    if nk > 1:
        scratch.append(pltpu.VMEM((tm, tn), F32))
    osz = jnp.dtype(out_dtype).itemsize
    vmem = (2 * tm * tk * 2 + 2 * tk * tn * 4 + nk * tk * tn * 2 + tm * tn * 4 * 2 + 2 * tm * tn * osz
            + sum(2 * tm * tn * a.dtype.itemsize for a, kind, _ in extras if kind == 'tile'))
    return pl.pallas_call(
        functools.partial(_mm_kernel, nk=nk, n_extra=len(extras), epilogue=epilogue),
        grid=grid,
        in_specs=in_specs,
        out_specs=pl.BlockSpec((tm, tn), lambda j, i, k: (i, j)),
        out_shape=jax.ShapeDtypeStruct((m, n_cols), out_dtype),
        scratch_shapes=scratch,
        compiler_params=_params(("parallel", "arbitrary", "arbitrary"), vmem),
        name=name,
    )(*args)


def _rms(x, w):
    return x * lax.rsqrt(jnp.mean(x * x, axis=-1, keepdims=True) + EPS) * w


def _prenorm_kernel(h_ref, w_ref, u_ref):
    u_ref[...] = _rms(h_ref[...], w_ref[...]).astype(u_ref.dtype)


def _prenorm(h, w_pre):
    m, d = h.shape
    tr = _tile(m, 256)
    return pl.pallas_call(
        _prenorm_kernel,
        grid=(m // tr,),
        in_specs=[pl.BlockSpec((tr, d), lambda i: (i, 0)), pl.BlockSpec((1, d), lambda i: (0, 0))],
        out_specs=pl.BlockSpec((tr, d), lambda i: (i, 0)),
        out_shape=jax.ShapeDtypeStruct((m, d), BF16),
        compiler_params=_params(("parallel",), 2 * tr * d * 6),
        name="prenorm",
    )(h, w_pre.reshape(1, d))


def _postnorm_kernel(h_ref, o_ref, wpost_ref, wpre_ref, hout_ref, u_ref, *, tr, n_pad):
    rows = pl.program_id(0) * tr + lax.broadcasted_iota(jnp.int32, (tr, 1), 0)
    h = h_ref[...] + _rms(o_ref[...].astype(F32), wpost_ref[...])
    h = jnp.where(rows >= n_pad, h, 0.0)
    hout_ref[...] = h
    u_ref[...] = _rms(h, wpre_ref[...]).astype(u_ref.dtype)


def _postnorm(h, o, w_post, w_pre, n_pad):
    m, d = h.shape
    tr = _tile(m, 256)
    row = pl.BlockSpec((tr, d), lambda i: (i, 0))
    vec = pl.BlockSpec((1, d), lambda i: (0, 0))
    return pl.pallas_call(
        functools.partial(_postnorm_kernel, tr=tr, n_pad=n_pad),
        grid=(m // tr,),
        in_specs=[row, row, vec, vec],
        out_specs=[row, row],
        out_shape=[jax.ShapeDtypeStruct((m, d), F32), jax.ShapeDtypeStruct((m, d), BF16)],
        compiler_params=_params(("parallel",), 2 * tr * d * (4 + 4 + 4 + 2)),
        name="postnorm",
    )(h, o, w_post.reshape(1, d), w_pre.reshape(1, d))


def _lastnorm_kernel(h_ref, o_ref, wpost_ref, y_ref):
    y_ref[...] = h_ref[...] + _rms(o_ref[...].astype(F32), wpost_ref[...])


def _lastnorm(h, o, w_post, row0, nrows):
    d = h.shape[1]
    tr = _tile(math.gcd(row0, nrows), 256)
    rb0 = row0 // tr
    return pl.pallas_call(
        _lastnorm_kernel,
        grid=(nrows // tr,),
        in_specs=[pl.BlockSpec((tr, d), lambda i: (rb0 + i, 0)), pl.BlockSpec((tr, d), lambda i: (rb0 + i, 0)),
                  pl.BlockSpec((1, d), lambda i: (0, 0))],
        out_specs=pl.BlockSpec((tr, d), lambda i: (i, 0)),
        out_shape=jax.ShapeDtypeStruct((nrows, d), F32),
        compiler_params=_params(("parallel",), 2 * tr * d * 12),
        name="lastnorm",
    )(h, o, w_post.reshape(1, d))


def _ssd_kernel(z_ref, x_ref, b_ref, c_ref, cwx_ref, cwb_ref, cwc_ref, cbx_ref, cbb_ref, cbc_ref,
                bufx_ref, bufb_ref, bufc_ref, dtc_ref, dtr_ref, dtbc_ref, dtbr_ref, ac_ref, ar_ref,
                dsk_ref, nw_ref, h0_ref, o_ref, hout_ref, xext, bext, cext, st, ybuf,
                *, q, hg, p, n_pad, nchunks):
    c = pl.program_id(2)
    gw = hg * p

    @pl.when(c == 0)
    def _():
        xext[0:SUBLANE, :] = bufx_ref[...]
        bext[0:SUBLANE, :] = bufb_ref[...]
        cext[0:SUBLANE, :] = bufc_ref[...]
        st[...] = h0_ref[...].T

    def conv_silu(ext, raw_ref, w_ref, bias_ref):
        ext[SUBLANE:SUBLANE + q, :] = raw_ref[...]
        acc = bias_ref[...]
        for tap in range(4):
            acc = acc + w_ref[tap:tap + 1, :] * ext[SUBLANE - 3 + tap:SUBLANE - 3 + tap + q, :]
        ext[0:SUBLANE, :] = ext[q:q + SUBLANE, :]
        return _silu(acc)

    xs = conv_silu(xext, x_ref, cwx_ref, cbx_ref)
    bc = conv_silu(bext, b_ref, cwb_ref, cbb_ref)
    cc = conv_silu(cext, c_ref, cwc_ref, cbc_ref)

    rows_c = c * q + lax.broadcasted_iota(jnp.int32, (q, 1), 0)
    rows_r = c * q + lax.broadcasted_iota(jnp.int32, (1, q), 1)
    dt_c = jnp.where(rows_c >= n_pad, _softplus(dtc_ref[...] + dtbc_ref[...]), 0.0)
    dt_r = jnp.where(rows_r >= n_pad, _softplus(dtr_ref[...] + dtbr_ref[...]), 0.0)
    ii = lax.broadcasted_iota(jnp.int32, (q, q), 0)
    jj = lax.broadcasted_iota(jnp.int32, (q, q), 1)
    tri = jj <= ii
    tril = jnp.where(tri, 1.0, 0.0).astype(BF16)
    triu = jnp.where(ii <= jj, 1.0, 0.0).astype(BF16)
    acum = _dot_f32_rhs(tril, dt_c * ac_ref[...])
    acum_t = _dot_f32_lhs(dt_r * ar_ref[...], triu)

    eh = lax.broadcasted_iota(jnp.int32, (LANE, gw), 0)
    el = lax.broadcasted_iota(jnp.int32, (LANE, gw), 1)
    expand = jnp.where(el // p == eh, 1.0, 0.0).astype(BF16)
    dt_x = _dot_f32_lhs(dt_c, expand)
    acum_x = _dot_f32_lhs(acum, expand)
    last_x = acum_x[q - 1:q, :]
    decay_in = jnp.exp(acum_x)
    decay_out = jnp.exp(last_x - acum_x)
    decay_all = jnp.exp(last_x)

    xdt = xs * dt_x
    xdt_b = xdt.astype(BF16)
    xw_b = (xdt * decay_out).astype(BF16)
    bc_t = bc.T.astype(BF16)
    cc_b = cc.astype(BF16)
    cb = _dot(cc_b, bc_t)
    lane_head = lax.broadcasted_iota(jnp.int32, (q, LANE), 1) // p
    heads_per_slab = LANE // p

    for s in range(gw // LANE):
        sl = slice(s * LANE, (s + 1) * LANE)
        y = _dot(cc_b, st[:, sl].astype(BF16)) * decay_in[:, sl]
        xd = xdt_b[:, sl]
        for r in range(heads_per_slab):
            hh = s * heads_per_slab + r
            seg = acum[:, hh:hh + 1] - acum_t[hh:hh + 1, :]
            dec = jnp.exp(jnp.where(tri, seg, NEG))
            mh = (cb * dec).astype(BF16)
            y = y + _dot(mh, jnp.where(lane_head == r, xd, jnp.zeros_like(xd)))
        st[:, sl] = st[:, sl] * decay_all[:, sl] + _dot(bc_t, xw_b[:, sl])
        ybuf[:, sl] = y + xs[:, sl] * dsk_ref[:, sl]

    g = ybuf[...] * _silu(z_ref[...])
    g = g * lax.rsqrt(jnp.mean(g * g, axis=-1, keepdims=True) + EPS)
    o_ref[...] = (g * nw_ref[...]).astype(o_ref.dtype)

    @pl.when(c == nchunks - 1)
    def _():
        hout_ref[...] = st[...].T


def _ssd_scan(zx, dt_c, dt_r, prm, conv_buf8, h0, *, q, nseq, nchunks, row0, n_pad, out_prev, m, name):
    g_, hg, p, n, inner = prm['G'], prm['hg'], prm['P'], prm['N'], prm['inner']
    gw = hg * p
    hgp = dt_r.shape[2]
    assert row0 % q == 0
    rb0 = row0 // q
    xo, bo, co = inner // gw, (2 * inner) // n, (2 * inner + g_ * n) // n
    cbo, cco = inner // n, (inner + g_ * n) // n

    def rmap(s, g, c):
        return rb0 + s * nchunks + c

    in_specs = [
        pl.BlockSpec((q, gw), lambda s, g, c: (rmap(s, g, c), g)),
        pl.BlockSpec((q, gw), lambda s, g, c: (rmap(s, g, c), xo + g)),
        pl.BlockSpec((q, n), lambda s, g, c: (rmap(s, g, c), bo + g)),
        pl.BlockSpec((q, n), lambda s, g, c: (rmap(s, g, c), co + g)),
        pl.BlockSpec((4, gw), lambda s, g, c: (0, g)),
        pl.BlockSpec((4, n), lambda s, g, c: (0, cbo + g)),
        pl.BlockSpec((4, n), lambda s, g, c: (0, cco + g)),
        pl.BlockSpec((1, gw), lambda s, g, c: (0, g)),
        pl.BlockSpec((1, n), lambda s, g, c: (0, cbo + g)),
        pl.BlockSpec((1, n), lambda s, g, c: (0, cco + g)),
        pl.BlockSpec((None, SUBLANE, gw), lambda s, g, c: (s, 0, g)),
        pl.BlockSpec((None, SUBLANE, n), lambda s, g, c: (s, 0, cbo + g)),
        pl.BlockSpec((None, SUBLANE, n), lambda s, g, c: (s, 0, cco + g)),
        pl.BlockSpec((None, q, LANE), lambda s, g, c: (g, rmap(s, g, c), 0)),
        pl.BlockSpec((None, None, hgp, q), lambda s, g, c: (g, s * nchunks + c, 0, 0)),
        pl.BlockSpec((None, 1, LANE), lambda s, g, c: (g, 0, 0)),
        pl.BlockSpec((None, hgp, 1), lambda s, g, c: (g, 0, 0)),
        pl.BlockSpec((None, 1, LANE), lambda s, g, c: (g, 0, 0)),
        pl.BlockSpec((None, hgp, 1), lambda s, g, c: (g, 0, 0)),
        pl.BlockSpec((1, gw), lambda s, g, c: (0, g)),
        pl.BlockSpec((1, gw), lambda s, g, c: (0, g)),
        pl.BlockSpec((None, None, gw, n), lambda s, g, c: (s, g, 0, 0)),
    ]
    args = [zx, zx, zx, zx, prm['conv_w'], prm['conv_w'], prm['conv_w'], prm['conv_b'], prm['conv_b'],
            prm['conv_b'], conv_buf8, conv_buf8, conv_buf8, dt_c, dt_r, prm['dtb_c'], prm['dtb_r'],
            prm['a_c'], prm['a_r'], prm['d_x'], prm['norm_w'], h0]
    aliases = {}
    if out_prev is not None:
        in_specs.append(pl.BlockSpec(memory_space=pl.ANY))
        args.append(out_prev)
        aliases = {len(args) - 1: 0}

    def body(*refs):
        if out_prev is not None:
            refs = refs[:22] + refs[23:]
        _ssd_kernel(*refs, q=q, hg=hg, p=p, n_pad=n_pad, nchunks=nchunks)

    vmem = (2 * (2 * q * gw * 4 + 2 * q * n * 4 + q * gw * 2 + 2 * n * gw * 4) + (2 * q + 8) * gw * 4
            + n * gw * 4 + 24 * q * gw * 4)
    return pl.pallas_call(
        body,
        grid=(nseq, g_, nchunks),
        in_specs=in_specs,
        out_specs=[pl.BlockSpec((q, gw), lambda s, g, c: (rmap(s, g, c), g)),
                   pl.BlockSpec((None, None, gw, n), lambda s, g, c: (s, g, 0, 0))],
        out_shape=[jax.ShapeDtypeStruct((m, inner), BF16),
                   jax.ShapeDtypeStruct((nseq, g_, gw, n), F32)],
        scratch_shapes=[pltpu.VMEM((q + SUBLANE, gw), F32), pltpu.VMEM((q + SUBLANE, n), F32),
                        pltpu.VMEM((q + SUBLANE, n), F32), pltpu.VMEM((n, gw), F32),
                        pltpu.VMEM((q, gw), F32)],
        input_output_aliases=aliases,
        compiler_params=_params(("parallel", "parallel", "arbitrary"), vmem),
        name=name,
    )(*args)


def _ssd_layer(u, lay, j, w_in, conv_w, conv_b, dt_bias, a_log, d_skip, norm_w, w_out, conv_state, ssd_state):
    m = u.shape[0]
    lp, nb, ls, n_pad = lay['lp'], lay['nb'], lay['ls'], lay['n_pad']
    h_, = dt_bias.shape
    inner = w_out.shape[1]
    p = inner // h_
    n = ssd_state.shape[-1]
    conv_dim = conv_w.shape[1]
    g_ = (conv_dim - inner) // (2 * n)
    hg = h_ // g_
    gw = hg * p
    assert LANE % p == 0 and gw % LANE == 0 and n % LANE == 0 and hg <= LANE
    hgp = -(-hg // SUBLANE) * SUBLANE

    zx = _mm(u, w_in, layer=j, n_cols=2 * inner + 2 * g_ * n, out_dtype=F32, name="ssd_in")
    dt_raw = _mm(u, w_in, layer=j, n_cols=h_, col_off=inner + conv_dim, out_dtype=F32, tn=min(h_, LANE),
                 name="ssd_dt")
    dt_c = jnp.pad(dt_raw.reshape(m, g_, hg).transpose(1, 0, 2), ((0, 0), (0, 0), (0, LANE - hg)))

    def dt_rows(q, start, stop):
        return dt_c[:, start:stop, :hgp].reshape(g_, (stop - start) // q, q, hgp).transpose(0, 1, 3, 2)

    a = -jnp.exp(a_log.astype(F32)).reshape(g_, hg)
    dtb = dt_bias.astype(F32).reshape(g_, hg)
    prm = dict(
        G=g_, hg=hg, P=p, N=n, inner=inner,
        conv_w=conv_w, conv_b=conv_b.reshape(1, conv_dim),
        dtb_c=jnp.pad(dtb, ((0, 0), (0, LANE - hg)))[:, None, :],
        dtb_r=jnp.pad(dtb, ((0, 0), (0, hgp - hg)))[:, :, None],
        a_c=jnp.pad(a, ((0, 0), (0, LANE - hg)))[:, None, :],
        a_r=jnp.pad(a, ((0, 0), (0, hgp - hg)))[:, :, None],
        d_x=jnp.repeat(d_skip.astype(F32), p).reshape(1, inner),
        norm_w=norm_w.reshape(1, inner),
    )

    def buf8(buf):
        return jnp.pad(buf.astype(F32), ((0, 0), (SUBLANE - 3, 0), (0, 0)))

    qp = 128
    y, hp = _ssd_scan(zx, dt_c, dt_rows(qp, 0, lp), prm, jnp.zeros((1, SUBLANE, conv_dim), F32),
                      jnp.zeros((1, g_, gw, n), F32), q=qp, nseq=1, nchunks=lp // qp, row0=0,
                      n_pad=n_pad, out_prev=None, m=m, name="ssd_scan_prompt")
    y, hs = _ssd_scan(zx, dt_c, dt_rows(ls, lp, m), prm, buf8(conv_state),
                      ssd_state.astype(F32).reshape(nb, g_, gw, n), q=ls, nseq=nb,
                      nchunks=1, row0=lp, n_pad=0, out_prev=y, m=m, name="ssd_scan_sample")
    o = _mm(y, w_out, layer=j, n_cols=w_out.shape[2], out_dtype=F32, name="ssd_out")

    xbc = zx[:, inner:]
    conv_p = xbc[lp - 3:lp][None]
    conv_s = xbc[lp:].reshape(nb, ls, conv_dim)[:, ls - 3:]
    return o, hp.reshape(1, h_, p, n), hs.reshape(nb, h_, p, n), conv_p, conv_s


def _cmul(pr, pi, xr, xi):
    return pr * xr - pi * xi, pr * xi + pi * xr


def _s5_kernel(v_ref, um_ref, km_ref, cm_ref, p_ref, d_ref, h0_ref, y_ref, hout_ref, carry, yscr,
               um_s, km_s, cm_s, *, rb, seg, nseg, sw, gl, s_, p_):
    t_ = S5_T

    @pl.when(pl.program_id(1) == 0)
    def _():
        carry[...] = h0_ref[...]

        def expand(c_ref, dst, row_div, col_div, inner):
            cw = c_ref.shape[1]
            n_out = dst.shape[1]
            r = lax.broadcasted_iota(jnp.int32, (cw, n_out), 0)
            c = lax.broadcasted_iota(jnp.int32, (cw, n_out), 1)
            rep = jnp.where((r // inner == c // (gl * inner)) & (r % inner == c % inner), 1.0, 0.0).astype(BF16)
            full = _dot(c_ref[...], rep)
            rr = lax.broadcasted_iota(jnp.int32, full.shape, 0)
            cc = lax.broadcasted_iota(jnp.int32, full.shape, 1)
            same = (rr // row_div) % gl == (cc // col_div) % gl
            dst[...] = jnp.where(same, full, 0.0).astype(BF16)

        expand(um_ref, um_s, s_, p_, p_)
        expand(km_ref, km_s, s_, s_, s_)
        expand(cm_ref, cm_s, p_, s_, s_)

    vs = [v_ref[pl.ds(i, rb, stride=t_), :] for i in range(t_)]
    vcat = jnp.concatenate(vs, axis=1).astype(BF16)
    x = _dot(vcat, um_s[...])
    rowk = lax.broadcasted_iota(jnp.int32, (rb, 1), 0) % seg
    hin = carry[...]
    if nseg == 1:
        hin_rows = jnp.broadcast_to(hin, (rb, 2 * sw))
    else:
        hin_rows = jnp.broadcast_to(hin[:, None, :], (nseg, seg, 2 * sw)).reshape(rb, 2 * sw)

    row8 = lax.broadcasted_iota(jnp.int32, (rb, 1), 0) % SUBLANE
    s = 1
    while s < SUBLANE:
        sh = jnp.where(row8 >= s, pltpu.roll(x, s, axis=0), 0.0)
        ar, ai = _cmul(p_ref[s:s + 1, :sw], p_ref[s:s + 1, sw:], sh[:, :sw], sh[:, sw:])
        x = x + jnp.concatenate([ar, ai], axis=1)
        s *= 2
    p8r, p8i = p_ref[1:SUBLANE + 1, :sw], p_ref[1:SUBLANE + 1, sw:]
    if seg == SUBLANE:
        ar, ai = _cmul(jnp.concatenate([p8r] * nseg, axis=0), jnp.concatenate([p8i] * nseg, axis=0),
                       hin_rows[:, :sw], hin_rows[:, sw:])
        x = x + jnp.concatenate([ar, ai], axis=1)
    else:
        groups = []
        prev = hin_rows[0:SUBLANE, :]
        for gi in range(seg // SUBLANE):
            ar, ai = _cmul(p8r, p8i, prev[:, :sw], prev[:, sw:])
            groups.append(x[gi * SUBLANE:(gi + 1) * SUBLANE, :] + jnp.concatenate([ar, ai], axis=1))
            prev = jnp.broadcast_to(groups[-1][SUBLANE - 1:SUBLANE, :], (SUBLANE, 2 * sw))
        x = jnp.concatenate(groups, axis=0)
    hb = jnp.where(rowk == 0, hin_rows, pltpu.roll(x, 1, axis=0))

    y_all = _dot(vcat, km_s[...]) + _dot(hb.astype(BF16), cm_s[...])
    for j in range(t_):
        yj = y_all[:, j * LANE:(j + 1) * LANE] + vs[j] * d_ref[...]
        yscr[pl.ds(j, rb, stride=t_), :] = _gelu_tanh(yj)
    y_ref[...] = yscr[...].astype(y_ref.dtype)

    if nseg == 1:
        xlast = x[seg - 1:seg, :]
    else:
        pick = (lax.broadcasted_iota(jnp.int32, (nseg, rb), 1)
                == lax.broadcasted_iota(jnp.int32, (nseg, rb), 0) * seg + (seg - 1))
        xlast = _dot_f32_rhs(jnp.where(pick, 1.0, 0.0).astype(BF16), x)
    carry[...] = xlast
    hout_ref[...] = xlast


def _s5_scan(vg, mats, ptab, d_row, h0, *, rb, seg, nseg, nblocks, m, name):
    um, km, cm = mats
    nt, tl, p2 = um.shape
    p_ = p2 // 2
    s_ = km.shape[2] // S5_T
    gl = LANE // s_
    sw = gl * p_
    sw2 = 2 * sw
    width = nt * LANE
    rows = rb * S5_T
    assert seg % SUBLANE == 0 and (nseg == 1 or seg == SUBLANE) and rb == nseg * seg
    in_specs = [
        pl.BlockSpec((rows, LANE), lambda t, r: (r, t)),
        pl.BlockSpec((None, tl, p2), lambda t, r: (t, 0, 0)),
        pl.BlockSpec((None, tl, km.shape[2]), lambda t, r: (t, 0, 0)),
        pl.BlockSpec((None, sw2, cm.shape[2]), lambda t, r: (t, 0, 0)),
        pl.BlockSpec((None, ptab.shape[1], sw2), lambda t, r: (t, 0, 0)),
        pl.BlockSpec((1, LANE), lambda t, r: (0, t)),
        pl.BlockSpec((None, nseg, sw2), lambda t, r: (t, 0, 0)),
    ]
    vmem = 2 * rows * LANE * 6 + (tl * sw2 + tl * tl + sw2 * tl) * 2 + 3 * tl * sw2 * 4 + 12 * rb * sw2 * 4
    return pl.pallas_call(
        functools.partial(_s5_kernel, rb=rb, seg=seg, nseg=nseg, sw=sw, gl=gl, s_=s_, p_=p_),
        grid=(nt, nblocks),
        in_specs=in_specs,
        out_specs=[pl.BlockSpec((rows, LANE), lambda t, r: (r, t)),
                   pl.BlockSpec((None, nseg, sw2), lambda t, r: (t, 0, 0))],
        out_shape=[jax.ShapeDtypeStruct((m, width), BF16),
                   jax.ShapeDtypeStruct((nt, nseg, sw2), F32)],
        scratch_shapes=[pltpu.VMEM((nseg, sw2), F32), pltpu.VMEM((rows, LANE), F32),
                        pltpu.VMEM((tl, sw2), BF16), pltpu.VMEM((tl, tl), BF16), pltpu.VMEM((sw2, tl), BF16)],
        compiler_params=_params(("parallel", "arbitrary"), vmem),
        name=name,
    )(vg, um, km, cm, ptab, d_row, h0)


def _s5_tables(lam_re, lam_im, log_step, b_re, b_im, c_re, c_im):
    t_ = S5_T
    g_, p = lam_re.shape
    s_ = b_re.shape[-1]
    gl = LANE // s_
    nt = g_ // gl
    lam = lax.complex(lam_re.astype(F32), lam_im.astype(F32))
    delta = jnp.exp(log_step.astype(F32))[:, None]
    lam_bar = jnp.exp(lam * delta)
    b_bar = ((lam_bar - 1.0) / lam)[..., None] * lax.complex(b_re.astype(F32), b_im.astype(F32))
    c_t = lax.complex(c_re.astype(F32), c_im.astype(F32))

    def power(d):
        d = jnp.asarray(d, F32)[..., None, None]
        return jnp.exp((lam * delta) * d)

    pw = power(jnp.arange(t_ + 1))
    ub = pw[t_ - 1 - jnp.arange(t_)][:, :, :, None] * b_bar[None]
    ub = ub.transpose(1, 0, 3, 2).reshape(nt, gl, t_, s_, p).transpose(0, 2, 1, 3, 4)
    um = jnp.stack([ub.real, ub.imag], axis=4).astype(BF16).reshape(nt, t_ * gl * s_, 2 * p)

    kd = jnp.einsum('gop,dgp,gpi->dgio', c_t, pw[:t_], b_bar).real
    ii = jnp.arange(t_)[:, None]
    jj = jnp.arange(t_)[None, :]
    kf = jnp.where((jj >= ii)[:, :, None, None, None], kd[jnp.clip(jj - ii, 0, t_ - 1)], 0.0)
    kf = kf.transpose(2, 0, 3, 1, 4).reshape(nt, gl, t_, s_, t_, s_).transpose(0, 2, 1, 3, 4, 5)
    km = kf.astype(BF16).reshape(nt, t_ * gl * s_, t_ * s_)

    cw = c_t[None] * pw[1:][:, :, None, :]
    cw = cw.transpose(1, 3, 0, 2).reshape(nt, gl, p, t_, s_)
    cm = jnp.stack([cw.real, -cw.imag], axis=1).astype(BF16).reshape(nt, 2 * gl * p, t_ * s_)

    nrow = 2 * SUBLANE
    pt = power(t_ * jnp.arange(nrow))
    pt = pt.reshape(nrow, nt, gl * p).transpose(1, 0, 2)
    ptab = jnp.concatenate([pt.real, pt.imag], axis=-1)
    return (um, km, cm), ptab


def _s5_layer(u, lay, w_in, lam_re, lam_im, log_step, b_re, b_im, c_re, c_im, d_skip, w_glu, b_glu, w_out,
              state):
    m = u.shape[0]
    lp, nb, ls = lay['lp'], lay['nb'], lay['ls']
    width = w_glu.shape[0]
    g_, p = lam_re.shape
    s_ = b_re.shape[-1]
    gl = LANE // s_
    nt = g_ // gl
    sw = gl * p
    assert g_ * s_ == width and LANE % s_ == 0 and ls % S5_T == 0 and lp % S5_T == 0

    vg = _mm(u, w_in, n_cols=2 * width, out_dtype=F32, name="s5_in")
    rb_p = _tile(lp // S5_T, 384)
    seg_s = ls // S5_T
    mats, ptab = _s5_tables(lam_re, lam_im, log_step, b_re, b_im, c_re, c_im)
    d_row = d_skip.astype(F32).reshape(1, width)

    def to_tiles(st):
        b = st.shape[0]
        return st.astype(F32).reshape(b, nt, gl, p, 2).transpose(1, 0, 4, 2, 3).reshape(nt, b, 2 * sw)

    def from_tiles(h):
        b = h.shape[1]
        return h.reshape(nt, b, 2, gl, p).transpose(1, 0, 3, 4, 2).reshape(b, g_, p, 2)

    y, hp = _s5_scan(vg, mats, ptab, d_row, jnp.zeros((nt, 1, 2 * sw), F32), rb=rb_p, seg=rb_p, nseg=1,
                     nblocks=lp // (rb_p * S5_T), m=m, name="s5_scan_prompt")
    y_s, hs = _s5_scan(vg[lp:], mats, ptab, d_row, to_tiles(state), rb=nb * seg_s, seg=seg_s, nseg=nb,
                       nblocks=1, m=nb * ls, name="s5_scan_sample")
    y = lax.dynamic_update_slice(y, y_s, (lp, 0))

    def glu(acc, yv, gate, bias):
        return yv.astype(F32) * _sigmoid(acc + bias) * _silu(gate)

    t = _mm(y, w_glu, n_cols=width, out_dtype=BF16, epilogue=glu,
            extras=((y, 'tile', 0), (vg, 'tile', width), (b_glu.reshape(1, width), 'row', 0)), name="s5_glu")
    o = _mm(t, w_out, n_cols=w_out.shape[1], out_dtype=F32, name="s5_out")
    return o, from_tiles(hp), from_tiles(hs)


def _rope_tile(t, cos_ref, sina_ref, sinb_ref, half):
    return (t * cos_ref[...] + pltpu.roll(t, LANE - half, axis=1) * sina_ref[...]
            + pltpu.roll(t, half, axis=1) * sinb_ref[...])


def _mla_prep_kernel(c_ref, qw_ref, kvw_ref, cos_ref, sina_ref, sinb_ref,
                     cqn_ref, ckv_ref, kr_ref, ckr_ref, *, qr, kvr, rope):
    c = c_ref[...]
    cqn_ref[...] = _rms(c[:, :qr], qw_ref[...]).astype(cqn_ref.dtype)
    ckv = _rms(c[:, qr:qr + kvr], kvw_ref[...])
    ckv_ref[...] = ckv
    rot = _rope_tile(c[:, qr + kvr:qr + kvr + LANE], cos_ref, sina_ref, sinb_ref, rope // 2)
    kr_ref[...] = rot[:, :rope]
    ckr_ref[...] = jnp.concatenate([ckv, rot], axis=1).astype(ckr_ref.dtype)


def _mla_prep(gc, col_off, wc, q_norm, kv_norm, tabs, *, qr, kvr, rope):
    m = gc.shape[0]
    tr = _tile(m, 256)
    assert col_off % wc == 0 and wc >= qr + kvr + LANE
    cb = col_off // wc
    row = lambda w: pl.BlockSpec((tr, w), lambda i: (i, 0))
    return pl.pallas_call(
        functools.partial(_mla_prep_kernel, qr=qr, kvr=kvr, rope=rope),
        grid=(m // tr,),
        in_specs=[pl.BlockSpec((tr, wc), lambda i: (i, cb)),
                  pl.BlockSpec((1, qr), lambda i: (0, 0)), pl.BlockSpec((1, kvr), lambda i: (0, 0)),
                  row(LANE), row(LANE), row(LANE)],
        out_specs=[row(qr), row(kvr), row(rope), row(kvr + LANE)],
        out_shape=[jax.ShapeDtypeStruct((m, qr), BF16), jax.ShapeDtypeStruct((m, kvr), F32),
                   jax.ShapeDtypeStruct((m, rope), F32), jax.ShapeDtypeStruct((m, kvr + LANE), BF16)],
        compiler_params=_params(("parallel",), 2 * tr * (wc * 4 + 3 * LANE * 4 + qr * 2 + kvr * 4 + (kvr + LANE) * 2)),
        name="mla_prep",
    )(gc, q_norm.reshape(1, qr), kv_norm.reshape(1, kvr), *tabs)


def _attn_tile(qs, k, v, m_i, l_i, acc, mask):
    s = lax.dot_general(qs, k, (((1,), (1,)), ((), ())), preferred_element_type=F32)
    if mask is not None:
        s = jnp.where(mask, s, NEG)
    m_new = jnp.maximum(m_i, jnp.max(s, axis=1, keepdims=True))
    alpha = jnp.exp(m_i - m_new)
    pexp = jnp.exp(s - m_new)
    l_new = alpha * l_i + jnp.sum(pexp, axis=1, keepdims=True)
    acc_new = alpha * acc + _dot(pexp.astype(BF16), v)
    return m_new, l_new, acc_new


def _q_scaled(q_ref, cos_ref, sina_ref, sinb_ref, *, nope, rope, scale):
    qf = q_ref[...].astype(F32)
    qrot = _rope_tile(qf[:, nope:], cos_ref, sina_ref, sinb_ref, rope // 2)
    return (jnp.concatenate([qf[:, :nope], qrot], axis=1) * scale).astype(BF16)


def _attn_prompt_kernel(q_ref, cos_ref, sina_ref, sinb_ref, k_ref, v_ref, g_ref, o_ref,
                        q_scr, s_buf, p_buf, a_buf, m_scr, l_scr, acc_scr, bias_buf,
                        *, tq, tk, nope, rope, dv, scale, n_pad, first):
    i = pl.program_id(1)
    q_scr[...] = _q_scaled(q_ref, cos_ref, sina_ref, sinb_ref, nope=nope, rope=rope,
                           scale=scale * math.log2(math.e))

    def chunk_of(r):
        return jnp.where(r < first, 0, 1 + (r - first) // CHUNK)

    qchunk = chunk_of(i * tq + lax.broadcasted_iota(jnp.int32, (tq, 1), 0))
    col = lax.broadcasted_iota(jnp.int32, (1, tk), 1)
    ja = (i * tq) // tk
    jl = (i * tq + tq - 1) // tk

    def bias_of(j):
        kcol = j * tk + col
        vis = jnp.where(kcol >= n_pad, chunk_of(kcol), jnp.iinfo(jnp.int32).max) <= qchunk
        return jnp.where(vis, 0.0, 2.0 * NEG)

    @pl.when(i == 0)
    def _():
        bias_buf[0] = jnp.zeros((tq, tk), F32)
        bias_buf[4] = jnp.full((tq, tk), 2.0 * NEG, F32)
        s_buf[1] = jnp.zeros((tq, tk), F32)

    @pl.when(i == -(-tk // tq))
    def _():
        bias_buf[1] = bias_of(0)

    bias_buf[2] = bias_of(ja)
    bias_buf[3] = bias_of(jl)
    p_buf[0] = jnp.zeros((tq, tk), BF16)
    a_buf[0] = jnp.ones((tq, LANE), F32)
    m_scr[...] = jnp.full((tq, LANE), NEG, F32)
    l_scr[...] = jnp.zeros((tq, LANE), F32)
    acc_scr[...] = jnp.zeros((tq, dv), F32)
    nl = tk // LANE

    def stage(t, sa, sb):
        ks = pl.multiple_of(jnp.minimum(t, jl) * tk, tk)
        s_buf[sa] = lax.dot_general(q_scr[...], k_ref[pl.ds(ks, tk), :], (((1,), (1,)), ((), ())),
                                    preferred_element_type=F32)
        vs = pl.multiple_of(jnp.clip(t - 2, 0, jl) * tk, tk)
        acc_scr[...] = a_buf[sa] * acc_scr[...] + _dot(p_buf[sa], v_ref[pl.ds(vs, tk), :])
        j = t - 1
        kind = jnp.where((j < 0) | (j > jl), 4,
                         jnp.where(j == jl, 3, jnp.where(j == ja, 2, jnp.where(j == 0, 1, 0))))
        s = s_buf[sb] + bias_buf[kind]
        m_old = m_scr[...]
        m_new = jnp.maximum(m_old, jnp.max(s, axis=1, keepdims=True))
        a_new = jnp.exp2(m_old - m_new)
        pf = jnp.exp2(s - jnp.concatenate([m_new] * nl, axis=1))
        part = pf[:, 0:LANE]
        for c in range(1, nl):
            part = part + pf[:, c * LANE:(c + 1) * LANE]
        l_scr[...] = a_new * l_scr[...] + part
        m_scr[...] = m_new
        a_buf[sb] = a_new
        p_buf[sb] = pf.astype(BF16)

    def four_stages(tt, carry):
        stage(4 * tt, 0, 1)
        stage(4 * tt + 1, 1, 0)
        stage(4 * tt + 2, 0, 1)
        stage(4 * tt + 3, 1, 0)
        return carry

    lax.fori_loop(0, (jl + 6) // 4, four_stages, 0)
    l_i = jnp.sum(l_scr[...], axis=1, keepdims=True)
    o_ref[...] = (acc_scr[...] / l_i * _silu(g_ref[...].astype(F32))).astype(o_ref.dtype)


def _attn_sample_kernel(q_ref, cos_ref, sina_ref, sinb_ref, k_ref, v_ref, g_ref, prev_ref, o_ref,
                        *, nope, rope, dv, scale, n_valid):
    del prev_ref
    qs = _q_scaled(q_ref, cos_ref, sina_ref, sinb_ref, nope=nope, rope=rope, scale=scale)
    lq, lk = qs.shape[0], k_ref.shape[0]
    mask = lax.broadcasted_iota(jnp.int32, (1, lk), 1) < n_valid
    carry = (jnp.full((lq, 1), NEG, F32), jnp.zeros((lq, 1), F32), jnp.zeros((lq, dv), F32))
    _, l_i, acc = _attn_tile(qs, k_ref[...], v_ref[...], *carry, mask)
    o_ref[...] = (acc / l_i * _silu(g_ref[...].astype(F32))).astype(o_ref.dtype)


def _mla_layer(u, lay, w_in, q_norm, w_uq, kv_norm, w_ukv, w_out, cache_ckv, cache_kr, cache_meta_ckv,
               cache_meta_kr):
    m, d = u.shape
    lp, nb, ls, n_pad, first, n_meta = lay['lp'], lay['nb'], lay['ls'], lay['n_pad'], lay['first'], lay['n_meta']
    qr, = q_norm.shape
    kvr, = kv_norm.shape
    rope = cache_kr.shape[-1]
    width = w_out.shape[0]
    heads = (w_uq.shape[1] - w_ukv.shape[1] + width) // rope
    nope = w_uq.shape[1] // heads - rope
    dv = width // heads
    past = cache_ckv.shape[1]
    assert nope == LANE and dv == LANE and 2 * rope == LANE and (qr + kvr) % LANE == 0 and n_meta > 0
    dk = nope + LANE
    scale = float(nope + rope) ** -0.5

    wc = min(c for c in range(qr + kvr + LANE, width + 1, LANE) if width % c == 0)
    w_perm = jnp.concatenate([w_in[:, qr + kvr + rope:], w_in[:, :qr + kvr + rope],
                              jnp.zeros((d, wc - (qr + kvr + rope)), w_in.dtype)], axis=1)
    ncol = width + wc
    gc = _mm(u, w_perm, n_cols=ncol, out_dtype=F32, tn=_tile(math.gcd(width, wc), 512, LANE), name="mla_in")

    pos = np.concatenate([np.maximum(np.arange(lp) - n_pad, 0),
                          np.tile(n_meta + past + np.arange(ls), nb)]).astype(np.float32)
    half = rope // 2
    inv = ROPE_THETA ** (-jnp.arange(half, dtype=F32) / half)
    ang = jnp.asarray(pos)[:, None] * inv[None, :]
    cos, sin, zero = jnp.cos(ang), jnp.sin(ang), jnp.zeros((m, LANE - rope), F32)
    zh = jnp.zeros((m, half), F32)
    tabs = (jnp.concatenate([cos, cos, zero], axis=1), jnp.concatenate([-sin, zh, zero], axis=1),
            jnp.concatenate([zh, sin, zero], axis=1))

    cqn, ckv, kr, ckr = _mla_prep(gc, width, wc, q_norm, kv_norm, tabs, qr=qr, kvr=kvr, rope=rope)

    wq = jnp.pad(w_uq.reshape(qr, heads, nope + rope), ((0, 0), (0, 0), (0, LANE - rope))).reshape(qr, heads * dk)
    wkv = w_ukv.reshape(kvr, heads, nope + dv)
    eye = jnp.pad(jnp.eye(rope, dtype=w_ukv.dtype), ((0, LANE - rope), (0, LANE - rope)))
    wk = jnp.concatenate([
        jnp.pad(wkv[:, :, :nope], ((0, 0), (0, 0), (0, LANE))),
        jnp.pad(jnp.broadcast_to(eye[:, None, :], (LANE, heads, LANE)), ((0, 0), (0, 0), (nope, 0)))],
        axis=0).reshape(kvr + LANE, heads * dk)
    wv = jnp.pad(wkv[:, :, nope:], ((0, LANE), (0, 0), (0, 0))).reshape(kvr + LANE, heads * dv)

    tq = _tile(lp, 384, LANE)
    tk = 4 * LANE
    lkv = -(-lp // tk) * tk
    if not (tq <= tk and lkv <= m):
        tk, lkv = tq, lp
    q = _mm(cqn, wq, n_cols=heads * dk, out_dtype=BF16, name="mla_q")
    k_p = _mm(ckr, wk, n_cols=heads * dk, out_dtype=BF16, rows=lkv, name="mla_k_prompt")
    v_p = _mm(ckr, wv, n_cols=heads * dv, out_dtype=BF16, rows=lkv, name="mla_v_prompt")

    s_chunk = 1 + (past + np.arange(ls)) // CHUNK
    k_chunk = np.concatenate([1 + np.arange(past) // CHUNK, s_chunk, np.zeros(n_meta, np.int64)])
    assert (k_chunk[None, :] <= s_chunk[:, None]).all()
    n_valid = past + ls + n_meta
    lk = -(-n_valid // LANE) * LANE

    def cat_rows(ckv_part, kr_part):
        return jnp.concatenate([ckv_part.astype(BF16), kr_part.astype(BF16),
                                jnp.zeros(ckv_part.shape[:-1] + (LANE - rope,), BF16)], axis=-1)

    ckr_s = jnp.concatenate([
        cat_rows(cache_ckv, cache_kr), ckr[lp:].reshape(nb, ls, kvr + LANE),
        cat_rows(cache_meta_ckv, cache_meta_kr), jnp.zeros((nb, lk - n_valid, kvr + LANE), BF16)],
        axis=1).reshape(nb * lk, kvr + LANE)
    k_s = _mm(ckr_s, wk, n_cols=heads * dk, out_dtype=BF16, name="mla_k_sample")
    v_s = _mm(ckr_s, wv, n_cols=heads * dv, out_dtype=BF16, name="mla_v_sample")

    assert first % CHUNK == 0 and tq % CHUNK == 0 and first <= tq <= tk and lkv >= 2 * tk
    kw = dict(nope=nope, rope=rope, dv=dv, scale=scale)
    qspec = lambda rows, f: [pl.BlockSpec((rows, dk), f), pl.BlockSpec((rows, LANE), lambda *a: (f(*a)[0], 0)),
                             pl.BlockSpec((rows, LANE), lambda *a: (f(*a)[0], 0)),
                             pl.BlockSpec((rows, LANE), lambda *a: (f(*a)[0], 0))]
    o = pl.pallas_call(
        functools.partial(_attn_prompt_kernel, tq=tq, tk=tk, n_pad=n_pad, first=first, **kw),
        grid=(heads, lp // tq),
        in_specs=qspec(tq, lambda h, i: (i, h)) + [
            pl.BlockSpec((lkv, dk), lambda h, i: (0, h)), pl.BlockSpec((lkv, dv), lambda h, i: (0, h)),
            pl.BlockSpec((tq, dv), lambda h, i: (i, h))],
        out_specs=pl.BlockSpec((tq, dv), lambda h, i: (i, h)),
        out_shape=jax.ShapeDtypeStruct((m, width), BF16),
        scratch_shapes=[pltpu.VMEM((tq, dk), BF16), pltpu.VMEM((2, tq, tk), F32), pltpu.VMEM((2, tq, tk), BF16),
                        pltpu.VMEM((2, tq, LANE), F32), pltpu.VMEM((tq, LANE), F32), pltpu.VMEM((tq, LANE), F32),
                        pltpu.VMEM((tq, dv), F32), pltpu.VMEM((5, tq, tk), F32)],
        compiler_params=_params(("parallel", "arbitrary"), 2 * lkv * (dk + dv) * 2 + 20 * tq * tk * 4),
        name="mla_attn_prompt",
    )(q, *tabs, k_p, v_p, gc)
    assert lp % ls == 0
    rb0 = lp // ls
    o = pl.pallas_call(
        functools.partial(_attn_sample_kernel, n_valid=n_valid, **kw),
        grid=(nb, heads),
        in_specs=qspec(ls, lambda b, h: (rb0 + b, h)) + [
            pl.BlockSpec((lk, dk), lambda b, h: (b, h)), pl.BlockSpec((lk, dv), lambda b, h: (b, h)),
            pl.BlockSpec((ls, dv), lambda b, h: (rb0 + b, h)), pl.BlockSpec(memory_space=pl.ANY)],
        out_specs=pl.BlockSpec((ls, dv), lambda b, h: (rb0 + b, h)),
        out_shape=jax.ShapeDtypeStruct((m, width), BF16),
        input_output_aliases={7: 0},
        compiler_params=_params(("parallel", "arbitrary"), 2 * lk * (dk + dv) * 2 + 8 * ls * lk * 4),
        name="mla_attn_sample",
    )(q, *tabs, k_s, v_s, gc, o)
    out = _mm(o, w_out, n_cols=w_out.shape[1], out_dtype=F32, name="mla_out")
    return out, ckv, kr


def kernel(x_prompt, x_sample, state_ssd, state_conv, state_s5, cache_ckv, cache_kr, cache_meta_ckv, cache_meta_kr, meta_tokens, norm_pre, norm_post, ssd_w_in, ssd_conv_w, ssd_conv_b, ssd_dt_bias, ssd_a_log, ssd_d, ssd_norm, ssd_w_out, s5_w_in, s5_lam_re, s5_lam_im, s5_log_step, s5_b_re, s5_b_im, s5_c_re, s5_c_im, s5_d, s5_w_glu, s5_b_glu, s5_w_out, mla_w_in, mla_q_norm, mla_w_uq, mla_kv_norm, mla_w_ukv, mla_w_out):
    bp, seq, d = x_prompt.shape
    nb, ls, _ = x_sample.shape
    n_meta = meta_tokens.shape[0]
    depth = norm_pre.shape[0]
    assert bp == 1 and seq % LANE == 0 and ls == CHUNK
    n_pad = (-n_meta) % LANE
    first = n_pad + n_meta
    lp = first + seq
    m = lp + nb * ls
    lay = dict(lp=lp, nb=nb, ls=ls, n_pad=n_pad, first=first, n_meta=n_meta)

    h = jnp.concatenate([jnp.zeros((n_pad, d), F32), meta_tokens.astype(F32), x_prompt[0].astype(F32),
                         x_sample.reshape(nb * ls, d).astype(F32)], axis=0)
    u = _prenorm(h, norm_pre[0])

    ssd_p, ssd_s, conv_p, conv_s, s5_p, s5_s = [], [], [], [], [], []
    ckv_all, kr_all = [], []
    for i in range(depth):
        kind, j = i % 3, i // 3
        if kind == 0:
            o, sp, ss, cp, cs = _ssd_layer(u, lay, j, ssd_w_in, ssd_conv_w[j], ssd_conv_b[j], ssd_dt_bias[j],
                                           ssd_a_log[j], ssd_d[j], ssd_norm[j], ssd_w_out, state_conv[j],
                                           state_ssd[j])
            ssd_p.append(sp.astype(state_ssd.dtype))
            ssd_s.append(ss.astype(state_ssd.dtype))
            conv_p.append(cp)
            conv_s.append(cs)
        elif kind == 1:
            o, hpp, hss = _s5_layer(u, lay, s5_w_in[j], s5_lam_re[j], s5_lam_im[j], s5_log_step[j], s5_b_re[j],
                                    s5_b_im[j], s5_c_re[j], s5_c_im[j], s5_d[j], s5_w_glu[j], s5_b_glu[j],
                                    s5_w_out[j], state_s5[j])
            s5_p.append(hpp.astype(state_s5.dtype))
            s5_s.append(hss.astype(state_s5.dtype))
        else:
            o, ckv, kr = _mla_layer(u, lay, mla_w_in[j], mla_q_norm[j], mla_w_uq[j], mla_kv_norm[j],
                                    mla_w_ukv[j], mla_w_out[j], cache_ckv[j], cache_kr[j], cache_meta_ckv[j],
                                    cache_meta_kr[j])
            ckv_all.append(ckv)
            kr_all.append(kr)
        if i + 1 < depth:
            h, u = _postnorm(h, o, norm_post[i], norm_pre[i + 1], n_pad)
        else:
            y_prompt = _lastnorm(h, o, norm_post[i], first, lp - first)[None]
            y_sample = _lastnorm(h, o, norm_post[i], lp, nb * ls).reshape(nb, ls, d)

    def split(a):
        return (a[:, None, n_pad:first], a[:, None, first:lp], a[:, lp:].reshape(a.shape[0], nb, ls, a.shape[-1]))

    mckv_p, ckv_p, ckv_s = split(jnp.stack(ckv_all))
    mkr_p, kr_p, kr_s = split(jnp.stack(kr_all))
    return (y_prompt, y_sample, jnp.stack(ssd_p), jnp.stack(ssd_s), jnp.stack(conv_p), jnp.stack(conv_s),
            jnp.stack(s5_p), jnp.stack(s5_s), mckv_p, mkr_p, ckv_p, kr_p, ckv_s, kr_s)
```

```python
import functools
import math

import numpy as np
import jax
import jax.numpy as jnp
from jax import lax
from jax.experimental import pallas as pl
from jax.experimental.pallas import tpu as pltpu

F32 = jnp.float32
BF16 = jnp.bfloat16

EPS = 1e-6
CHUNK = 64
ROPE_THETA = 10000.0

LANE = 128
SUBLANE = 8
VMEM_CAP = 56 * 1024 * 1024
W_BLOCK_BYTES = 8 * 1024 * 1024
NEG = -1e30
S5_T = 8


def _tile(dim, target, align=SUBLANE):
    best = None
    for t in range(align, min(dim, target) + 1, align):
        if dim % t == 0:
            best = t
    assert best is not None, (dim, target, align)
    return best


def _params(sem, vmem_bytes):
    limit = int(min(VMEM_CAP, max(vmem_bytes * 5 // 4 + (4 << 20), 16 << 20)))
    return pltpu.CompilerParams(dimension_semantics=sem, vmem_limit_bytes=limit)


def _sigmoid(x):
    return 0.5 + 0.5 * jnp.tanh(0.5 * x)


def _silu(x):
    h = 0.5 * x
    return h + h * jnp.tanh(h)


def _softplus(x):
    return jnp.maximum(x, 0.0) + jnp.log1p(jnp.exp(-jnp.abs(x)))


def _gelu_tanh(x):
    return 0.5 * x * (1.0 + jnp.tanh(math.sqrt(2.0 / math.pi) * (x + 0.044715 * (x * x * x))))


def _split3(x):
    hi = x.astype(BF16)
    r1 = x - hi.astype(F32)
    mid = r1.astype(BF16)
    lo = (r1 - mid.astype(F32)).astype(BF16)
    return hi, mid, lo


def _dot(a, b):
    return jnp.dot(a, b, preferred_element_type=F32)


def _dot_f32_lhs(x, e):
    hi, mid, lo = _split3(x)
    return _dot(hi, e) + _dot(mid, e) + _dot(lo, e)


def _dot_f32_rhs(e, x):
    hi, mid, lo = _split3(x)
    return _dot(e, hi) + _dot(e, mid) + _dot(e, lo)


def _mm_kernel(*refs, nk, n_extra, epilogue):
    x_ref, w_ref = refs[0], refs[1]
    extra = refs[2:2 + n_extra]
    o_ref = refs[2 + n_extra]
    wb_ref = refs[3 + n_extra]
    i = pl.program_id(1)
    k = pl.program_id(2)

    @pl.when(i == 0)
    def _():
        wb_ref[k] = w_ref[...].astype(BF16)

    part = _dot(x_ref[...], wb_ref[k])

    def finish(acc):
        vals = [e[...] for e in extra]
        o_ref[...] = (epilogue(acc, *vals) if epilogue is not None else acc).astype(o_ref.dtype)

    if nk == 1:
        finish(part)
    else:
        acc_ref = refs[4 + n_extra]

        @pl.when(k == 0)
        def _():
            acc_ref[...] = part

        @pl.when(k > 0)
        def _():
            acc_ref[...] += part

        @pl.when(k == nk - 1)
        def _():
            finish(acc_ref[...])


def _mm(x, w, *, n_cols, col_off=0, out_dtype, rows=None, tn=None, tm_target=1024, tk_target=4096,
        epilogue=None, extras=(), layer=None, name):
    m = x.shape[0] if rows is None else rows
    kdim = x.shape[1]
    assert w.shape[-2] == kdim and x.dtype == BF16 and w.ndim == (2 if layer is None else 3)
    tm = _tile(m, tm_target, 16)
    tk = _tile(kdim, tk_target, LANE)
    if tn is None:
        cap = max(LANE, min(2048, W_BLOCK_BYTES // (4 * tk)))
        tn = _tile(math.gcd(n_cols, col_off) if col_off else n_cols, cap, LANE)
    assert n_cols % tn == 0 and col_off % tn == 0, (n_cols, col_off, tn)
    nk = kdim // tk
    cb = col_off // tn
    grid = (n_cols // tn, m // tm, nk)

    def w_map(j, i, k):
        return (jnp.where(i == 0, k, nk - 1), j + cb)

    if layer is None:
        w_spec = pl.BlockSpec((tk, tn), w_map)
    else:
        w_spec = pl.BlockSpec((None, tk, tn), lambda j, i, k: (layer,) + w_map(j, i, k))
    in_specs = [pl.BlockSpec((tm, tk), lambda j, i, k: (i, k)), w_spec]
    args = [x, w]
    for arr, kind, off in extras:
        ob = off // tn
        assert off % tn == 0
        if kind == 'tile':
            in_specs.append(pl.BlockSpec((tm, tn), lambda j, i, k, ob=ob: (i, j + ob)))
        else:
            in_specs.append(pl.BlockSpec((1, tn), lambda j, i, k, ob=ob: (0, j + ob)))
        args.append(arr)
    scratch = [pltpu.VMEM((nk, tk, tn), BF16)]
    if nk > 1:
        scratch.append(pltpu.VMEM((tm, tn), F32))
    osz = jnp.dtype(out_dtype).itemsize
    vmem = (2 * tm * tk * 2 + 2 * tk * tn * 4 + nk * tk * tn * 2 + tm * tn * 4 * 2 + 2 * tm * tn * osz
            + sum(2 * tm * tn * a.dtype.itemsize for a, kind, _ in extras if kind == 'tile'))
    return pl.pallas_call(
        functools.partial(_mm_kernel, nk=nk, n_extra=len(extras), epilogue=epilogue),
        grid=grid,
        in_specs=in_specs,
        out_specs=pl.BlockSpec((tm, tn), lambda j, i, k: (i, j)),
        out_shape=jax.ShapeDtypeStruct((m, n_cols), out_dtype),
        scratch_shapes=scratch,
        compiler_params=_params(("parallel", "arbitrary", "arbitrary"), vmem),
        name=name,
    )(*args)


def _rms(x, w):
    return x * lax.rsqrt(jnp.mean(x * x, axis=-1, keepdims=True) + EPS) * w


def _prenorm_kernel(h_ref, w_ref, u_ref):
    u_ref[...] = _rms(h_ref[...], w_ref[...]).astype(u_ref.dtype)


def _prenorm(h, w_pre):
    m, d = h.shape
    tr = _tile(m, 256)
    return pl.pallas_call(
        _prenorm_kernel,
        grid=(m // tr,),
        in_specs=[pl.BlockSpec((tr, d), lambda i: (i, 0)), pl.BlockSpec((1, d), lambda i: (0, 0))],
        out_specs=pl.BlockSpec((tr, d), lambda i: (i, 0)),
        out_shape=jax.ShapeDtypeStruct((m, d), BF16),
        compiler_params=_params(("parallel",), 2 * tr * d * 6),
        name="prenorm",
    )(h, w_pre.reshape(1, d))


def _postnorm_kernel(h_ref, o_ref, wpost_ref, wpre_ref, hout_ref, u_ref, *, tr, n_pad):
    rows = pl.program_id(0) * tr + lax.broadcasted_iota(jnp.int32, (tr, 1), 0)
    h = h_ref[...] + _rms(o_ref[...].astype(F32), wpost_ref[...])
    h = jnp.where(rows >= n_pad, h, 0.0)
    hout_ref[...] = h
    u_ref[...] = _rms(h, wpre_ref[...]).astype(u_ref.dtype)


def _postnorm(h, o, w_post, w_pre, n_pad):
    m, d = h.shape
    tr = _tile(m, 256)
    row = pl.BlockSpec((tr, d), lambda i: (i, 0))
    vec = pl.BlockSpec((1, d), lambda i: (0, 0))
    return pl.pallas_call(
        functools.partial(_postnorm_kernel, tr=tr, n_pad=n_pad),
        grid=(m // tr,),
        in_specs=[row, row, vec, vec],
        out_specs=[row, row],
        out_shape=[jax.ShapeDtypeStruct((m, d), F32), jax.ShapeDtypeStruct((m, d), BF16)],
        compiler_params=_params(("parallel",), 2 * tr * d * (4 + 4 + 4 + 2)),
        name="postnorm",
    )(h, o, w_post.reshape(1, d), w_pre.reshape(1, d))


def _lastnorm_kernel(h_ref, o_ref, wpost_ref, y_ref):
    y_ref[...] = h_ref[...] + _rms(o_ref[...].astype(F32), wpost_ref[...])


def _lastnorm(h, o, w_post, row0, nrows):
    d = h.shape[1]
    tr = _tile(math.gcd(row0, nrows), 256)
    rb0 = row0 // tr
    return pl.pallas_call(
        _lastnorm_kernel,
        grid=(nrows // tr,),
        in_specs=[pl.BlockSpec((tr, d), lambda i: (rb0 + i, 0)), pl.BlockSpec((tr, d), lambda i: (rb0 + i, 0)),
                  pl.BlockSpec((1, d), lambda i: (0, 0))],
        out_specs=pl.BlockSpec((tr, d), lambda i: (i, 0)),
        out_shape=jax.ShapeDtypeStruct((nrows, d), F32),
        compiler_params=_params(("parallel",), 2 * tr * d * 12),
        name="lastnorm",
    )(h, o, w_post.reshape(1, d))


def _ssd_kernel(z_ref, x_ref, b_ref, c_ref, cwx_ref, cwb_ref, cwc_ref, cbx_ref, cbb_ref, cbc_ref,
                bufx_ref, bufb_ref, bufc_ref, dtc_ref, dtr_ref, dtbc_ref, dtbr_ref, ac_ref, ar_ref,
                dsk_ref, nw_ref, h0_ref, o_ref, hout_ref, xext, bext, cext, st, ybuf,
                *, q, hg, p, n_pad, nchunks):
    c = pl.program_id(2)
    gw = hg * p

    @pl.when(c == 0)
    def _():
        xext[0:SUBLANE, :] = bufx_ref[...]
        bext[0:SUBLANE, :] = bufb_ref[...]
        cext[0:SUBLANE, :] = bufc_ref[...]
        st[...] = h0_ref[...].T

    def conv_silu(ext, raw_ref, w_ref, bias_ref):
        ext[SUBLANE:SUBLANE + q, :] = raw_ref[...]
        e = ext[...]
        z = w_ref[0:1, :] * e
        for tap in range(1, 4):
            z = pltpu.roll(z, 1, axis=0) + w_ref[tap:tap + 1, :] * e
        ext[0:SUBLANE, :] = ext[q:q + SUBLANE, :]
        return _silu(z[SUBLANE:, :] + bias_ref[...])

    xs = conv_silu(xext, x_ref, cwx_ref, cbx_ref)
    bc = conv_silu(bext, b_ref, cwb_ref, cbb_ref)
    cc = conv_silu(cext, c_ref, cwc_ref, cbc_ref)

    rows_c = c * q + lax.broadcasted_iota(jnp.int32, (q, 1), 0)
    rows_r = c * q + lax.broadcasted_iota(jnp.int32, (1, q), 1)
    dt_c = jnp.where(rows_c >= n_pad, _softplus(dtc_ref[...] + dtbc_ref[...]), 0.0)
    dt_r = jnp.where(rows_r >= n_pad, _softplus(dtr_ref[...] + dtbr_ref[...]), 0.0)
    ii = lax.broadcasted_iota(jnp.int32, (q, q), 0)
    jj = lax.broadcasted_iota(jnp.int32, (q, q), 1)
    tri = jj <= ii
    tril = jnp.where(tri, 1.0, 0.0).astype(BF16)
    triu = jnp.where(ii <= jj, 1.0, 0.0).astype(BF16)
    acum = _dot_f32_rhs(tril, dt_c * ac_ref[...])
    acum_t = _dot_f32_lhs(dt_r * ar_ref[...], triu)

    eh = lax.broadcasted_iota(jnp.int32, (LANE, gw), 0)
    el = lax.broadcasted_iota(jnp.int32, (LANE, gw), 1)
    expand = jnp.where(el // p == eh, 1.0, 0.0).astype(BF16)
    dt_x = _dot_f32_lhs(dt_c, expand)
    acum_x = _dot_f32_lhs(acum, expand)
    last_x = acum_x[q - 1:q, :]
    decay_in = jnp.exp(acum_x)
    decay_out = jnp.exp(last_x - acum_x)
    decay_all = jnp.exp(last_x)

    xdt = xs * dt_x
    xdt_b = xdt.astype(BF16)
    xw_b = (xdt * decay_out).astype(BF16)
    bc_t = bc.T.astype(BF16)
    cc_b = cc.astype(BF16)
    cb = _dot(cc_b, bc_t)
    lane_head = lax.broadcasted_iota(jnp.int32, (q, LANE), 1) // p
    heads_per_slab = LANE // p

    for s in range(gw // LANE):
        sl = slice(s * LANE, (s + 1) * LANE)
        y = _dot(cc_b, st[:, sl].astype(BF16)) * decay_in[:, sl]
        xd = xdt_b[:, sl]
        for r in range(heads_per_slab):
            hh = s * heads_per_slab + r
            seg = acum[:, hh:hh + 1] - acum_t[hh:hh + 1, :]
            dec = jnp.exp(jnp.where(tri, seg, NEG))
            mh = (cb * dec).astype(BF16)
            y = y + _dot(mh, jnp.where(lane_head == r, xd, jnp.zeros_like(xd)))
        st[:, sl] = st[:, sl] * decay_all[:, sl] + _dot(bc_t, xw_b[:, sl])
        ybuf[:, sl] = y + xs[:, sl] * dsk_ref[:, sl]

    g = ybuf[...] * _silu(z_ref[...])
    g = g * lax.rsqrt(jnp.mean(g * g, axis=-1, keepdims=True) + EPS)
    o_ref[...] = (g * nw_ref[...]).astype(o_ref.dtype)

    @pl.when(c == nchunks - 1)
    def _():
        hout_ref[...] = st[...].T


def _ssd_scan(zx, dt_c, dt_r, prm, conv_buf8, h0, *, q, nseq, nchunks, row0, n_pad, out_prev, m, name):
    g_, hg, p, n, inner = prm['G'], prm['hg'], prm['P'], prm['N'], prm['inner']
    gw = hg * p
    hgp = dt_r.shape[2]
    assert row0 % q == 0
    rb0 = row0 // q
    xo, bo, co = inner // gw, (2 * inner) // n, (2 * inner + g_ * n) // n
    cbo, cco = inner // n, (inner + g_ * n) // n

    def rmap(s, g, c):
        return rb0 + s * nchunks + c

    in_specs = [
        pl.BlockSpec((q, gw), lambda s, g, c: (rmap(s, g, c), g)),
        pl.BlockSpec((q, gw), lambda s, g, c: (rmap(s, g, c), xo + g)),
        pl.BlockSpec((q, n), lambda s, g, c: (rmap(s, g, c), bo + g)),
        pl.BlockSpec((q, n), lambda s, g, c: (rmap(s, g, c), co + g)),
        pl.BlockSpec((4, gw), lambda s, g, c: (0, g)),
        pl.BlockSpec((4, n), lambda s, g, c: (0, cbo + g)),
        pl.BlockSpec((4, n), lambda s, g, c: (0, cco + g)),
        pl.BlockSpec((1, gw), lambda s, g, c: (0, g)),
        pl.BlockSpec((1, n), lambda s, g, c: (0, cbo + g)),
        pl.BlockSpec((1, n), lambda s, g, c: (0, cco + g)),
        pl.BlockSpec((None, SUBLANE, gw), lambda s, g, c: (s, 0, g)),
        pl.BlockSpec((None, SUBLANE, n), lambda s, g, c: (s, 0, cbo + g)),
        pl.BlockSpec((None, SUBLANE, n), lambda s, g, c: (s, 0, cco + g)),
        pl.BlockSpec((None, q, LANE), lambda s, g, c: (g, rmap(s, g, c), 0)),
        pl.BlockSpec((None, None, hgp, q), lambda s, g, c: (g, s * nchunks + c, 0, 0)),
        pl.BlockSpec((None, 1, LANE), lambda s, g, c: (g, 0, 0)),
        pl.BlockSpec((None, hgp, 1), lambda s, g, c: (g, 0, 0)),
        pl.BlockSpec((None, 1, LANE), lambda s, g, c: (g, 0, 0)),
        pl.BlockSpec((None, hgp, 1), lambda s, g, c: (g, 0, 0)),
        pl.BlockSpec((1, gw), lambda s, g, c: (0, g)),
        pl.BlockSpec((1, gw), lambda s, g, c: (0, g)),
        pl.BlockSpec((None, None, gw, n), lambda s, g, c: (s, g, 0, 0)),
    ]
    args = [zx, zx, zx, zx, prm['conv_w'], prm['conv_w'], prm['conv_w'], prm['conv_b'], prm['conv_b'],
            prm['conv_b'], conv_buf8, conv_buf8, conv_buf8, dt_c, dt_r, prm['dtb_c'], prm['dtb_r'],
            prm['a_c'], prm['a_r'], prm['d_x'], prm['norm_w'], h0]
    aliases = {}
    if out_prev is not None:
        in_specs.append(pl.BlockSpec(memory_space=pl.ANY))
        args.append(out_prev)
        aliases = {len(args) - 1: 0}

    def body(*refs):
        if out_prev is not None:
            refs = refs[:22] + refs[23:]
        _ssd_kernel(*refs, q=q, hg=hg, p=p, n_pad=n_pad, nchunks=nchunks)

    vmem = (2 * (2 * q * gw * 4 + 2 * q * n * 4 + q * gw * 2 + 2 * n * gw * 4) + (2 * q + 8) * gw * 4
            + n * gw * 4 + 24 * q * gw * 4)
    return pl.pallas_call(
        body,
        grid=(nseq, g_, nchunks),
        in_specs=in_specs,
        out_specs=[pl.BlockSpec((q, gw), lambda s, g, c: (rmap(s, g, c), g)),
                   pl.BlockSpec((None, None, gw, n), lambda s, g, c: (s, g, 0, 0))],
        out_shape=[jax.ShapeDtypeStruct((m, inner), BF16),
                   jax.ShapeDtypeStruct((nseq, g_, gw, n), F32)],
        scratch_shapes=[pltpu.VMEM((q + SUBLANE, gw), F32), pltpu.VMEM((q + SUBLANE, n), F32),
                        pltpu.VMEM((q + SUBLANE, n), F32), pltpu.VMEM((n, gw), F32),
                        pltpu.VMEM((q, gw), F32)],
        input_output_aliases=aliases,
        compiler_params=_params(("parallel", "parallel", "arbitrary"), vmem),
        name=name,
    )(*args)


def _ssd_layer(u, lay, j, w_in, conv_w, conv_b, dt_bias, a_log, d_skip, norm_w, w_out, conv_state, ssd_state):
    m = u.shape[0]
    lp, nb, ls, n_pad = lay['lp'], lay['nb'], lay['ls'], lay['n_pad']
    h_, = dt_bias.shape
    inner = w_out.shape[1]
    p = inner // h_
    n = ssd_state.shape[-1]
    conv_dim = conv_w.shape[1]
    g_ = (conv_dim - inner) // (2 * n)
    hg = h_ // g_
    gw = hg * p
    assert LANE % p == 0 and gw % LANE == 0 and n % LANE == 0 and hg <= LANE
    hgp = -(-hg // SUBLANE) * SUBLANE

    zx = _mm(u, w_in, layer=j, n_cols=2 * inner + 2 * g_ * n, out_dtype=F32, name="ssd_in")
    dt_raw = _mm(u, w_in, layer=j, n_cols=h_, col_off=inner + conv_dim, out_dtype=F32, tn=min(h_, LANE),
                 name="ssd_dt")
    dt_c = jnp.pad(dt_raw.reshape(m, g_, hg).transpose(1, 0, 2), ((0, 0), (0, 0), (0, LANE - hg)))

    def dt_rows(q, start, stop):
        return dt_c[:, start:stop, :hgp].reshape(g_, (stop - start) // q, q, hgp).transpose(0, 1, 3, 2)

    a = -jnp.exp(a_log.astype(F32)).reshape(g_, hg)
    dtb = dt_bias.astype(F32).reshape(g_, hg)
    prm = dict(
        G=g_, hg=hg, P=p, N=n, inner=inner,
        conv_w=conv_w, conv_b=conv_b.reshape(1, conv_dim),
        dtb_c=jnp.pad(dtb, ((0, 0), (0, LANE - hg)))[:, None, :],
        dtb_r=jnp.pad(dtb, ((0, 0), (0, hgp - hg)))[:, :, None],
        a_c=jnp.pad(a, ((0, 0), (0, LANE - hg)))[:, None, :],
        a_r=jnp.pad(a, ((0, 0), (0, hgp - hg)))[:, :, None],
        d_x=jnp.repeat(d_skip.astype(F32), p).reshape(1, inner),
        norm_w=norm_w.reshape(1, inner),
    )

    def buf8(buf):
        return jnp.pad(buf.astype(F32), ((0, 0), (SUBLANE - 3, 0), (0, 0)))

    qp = 128
    y, hp = _ssd_scan(zx, dt_c, dt_rows(qp, 0, lp), prm, jnp.zeros((1, SUBLANE, conv_dim), F32),
                      jnp.zeros((1, g_, gw, n), F32), q=qp, nseq=1, nchunks=lp // qp, row0=0,
                      n_pad=n_pad, out_prev=None, m=m, name="ssd_scan_prompt")
    y, hs = _ssd_scan(zx, dt_c, dt_rows(ls, lp, m), prm, buf8(conv_state),
                      ssd_state.astype(F32).reshape(nb, g_, gw, n), q=ls, nseq=nb,
                      nchunks=1, row0=lp, n_pad=0, out_prev=y, m=m, name="ssd_scan_sample")
    o = _mm(y, w_out, layer=j, n_cols=w_out.shape[2], out_dtype=F32, name="ssd_out")

    xbc = zx[:, inner:]
    conv_p = xbc[lp - 3:lp][None]
    conv_s = xbc[lp:].reshape(nb, ls, conv_dim)[:, ls - 3:]
    return o, hp.reshape(1, h_, p, n), hs.reshape(nb, h_, p, n), conv_p, conv_s


def _cmul(pr, pi, xr, xi):
    return pr * xr - pi * xi, pr * xi + pi * xr


def _s5_kernel(v_ref, um_ref, km_ref, cm_ref, p_ref, d_ref, h0_ref, y_ref, hout_ref, carry, yscr,
               um_s, km_s, cm_s, *, rb, seg, nseg, sw, gl, s_, p_):
    t_ = S5_T

    @pl.when(pl.program_id(1) == 0)
    def _():
        carry[...] = h0_ref[...]

        def expand(c_ref, dst, row_div, col_div, inner):
            cw = c_ref.shape[1]
            n_out = dst.shape[1]
            r = lax.broadcasted_iota(jnp.int32, (cw, n_out), 0)
            c = lax.broadcasted_iota(jnp.int32, (cw, n_out), 1)
            rep = jnp.where((r // inner == c // (gl * inner)) & (r % inner == c % inner), 1.0, 0.0).astype(BF16)
            full = _dot(c_ref[...], rep)
            rr = lax.broadcasted_iota(jnp.int32, full.shape, 0)
            cc = lax.broadcasted_iota(jnp.int32, full.shape, 1)
            same = (rr // row_div) % gl == (cc // col_div) % gl
            dst[...] = jnp.where(same, full, 0.0).astype(BF16)

        expand(um_ref, um_s, s_, p_, p_)
        expand(km_ref, km_s, s_, s_, s_)
        expand(cm_ref, cm_s, p_, s_, s_)

    vs = [v_ref[pl.ds(i, rb, stride=t_), :] for i in range(t_)]
    vcat = jnp.concatenate(vs, axis=1).astype(BF16)
    x = _dot(vcat, um_s[...])
    rowk = lax.broadcasted_iota(jnp.int32, (rb, 1), 0) % seg
    hin = carry[...]
    if nseg == 1:
        hin_rows = jnp.broadcast_to(hin, (rb, 2 * sw))
    else:
        hin_rows = jnp.broadcast_to(hin[:, None, :], (nseg, seg, 2 * sw)).reshape(rb, 2 * sw)

    row8 = lax.broadcasted_iota(jnp.int32, (rb, 1), 0) % SUBLANE
    s = 1
    while s < SUBLANE:
        sh = jnp.where(row8 >= s, pltpu.roll(x, s, axis=0), 0.0)
        ar, ai = _cmul(p_ref[s:s + 1, :sw], p_ref[s:s + 1, sw:], sh[:, :sw], sh[:, sw:])
        x = x + jnp.concatenate([ar, ai], axis=1)
        s *= 2
    p8r, p8i = p_ref[1:SUBLANE + 1, :sw], p_ref[1:SUBLANE + 1, sw:]
    if seg == SUBLANE:
        ar, ai = _cmul(jnp.concatenate([p8r] * nseg, axis=0), jnp.concatenate([p8i] * nseg, axis=0),
                       hin_rows[:, :sw], hin_rows[:, sw:])
        x = x + jnp.concatenate([ar, ai], axis=1)
    else:
        groups = []
        prev = hin_rows[0:SUBLANE, :]
        for gi in range(seg // SUBLANE):
            ar, ai = _cmul(p8r, p8i, prev[:, :sw], prev[:, sw:])
            groups.append(x[gi * SUBLANE:(gi + 1) * SUBLANE, :] + jnp.concatenate([ar, ai], axis=1))
            prev = jnp.broadcast_to(groups[-1][SUBLANE - 1:SUBLANE, :], (SUBLANE, 2 * sw))
        x = jnp.concatenate(groups, axis=0)
    hb = jnp.where(rowk == 0, hin_rows, pltpu.roll(x, 1, axis=0))

    y_all = _dot(vcat, km_s[...]) + _dot(hb.astype(BF16), cm_s[...])
    for j in range(t_):
        yj = y_all[:, j * LANE:(j + 1) * LANE] + vs[j] * d_ref[...]
        yscr[pl.ds(j, rb, stride=t_), :] = _gelu_tanh(yj)
    y_ref[...] = yscr[...].astype(y_ref.dtype)

    if nseg == 1:
        xlast = x[seg - 1:seg, :]
    else:
        pick = (lax.broadcasted_iota(jnp.int32, (nseg, rb), 1)
                == lax.broadcasted_iota(jnp.int32, (nseg, rb), 0) * seg + (seg - 1))
        xlast = _dot_f32_rhs(jnp.where(pick, 1.0, 0.0).astype(BF16), x)
    carry[...] = xlast
    hout_ref[...] = xlast


def _s5_scan(vg, mats, ptab, d_row, h0, *, rb, seg, nseg, nblocks, m, name):
    um, km, cm = mats
    nt, tl, p2 = um.shape
    p_ = p2 // 2
    s_ = km.shape[2] // S5_T
    gl = LANE // s_
    sw = gl * p_
    sw2 = 2 * sw
    width = nt * LANE
    rows = rb * S5_T
    assert seg % SUBLANE == 0 and (nseg == 1 or seg == SUBLANE) and rb == nseg * seg
    in_specs = [
        pl.BlockSpec((rows, LANE), lambda t, r: (r, t)),
        pl.BlockSpec((None, tl, p2), lambda t, r: (t, 0, 0)),
        pl.BlockSpec((None, tl, km.shape[2]), lambda t, r: (t, 0, 0)),
        pl.BlockSpec((None, sw2, cm.shape[2]), lambda t, r: (t, 0, 0)),
        pl.BlockSpec((None, ptab.shape[1], sw2), lambda t, r: (t, 0, 0)),
        pl.BlockSpec((1, LANE), lambda t, r: (0, t)),
        pl.BlockSpec((None, nseg, sw2), lambda t, r: (t, 0, 0)),
    ]
    vmem = 2 * rows * LANE * 6 + (tl * sw2 + tl * tl + sw2 * tl) * 2 + 3 * tl * sw2 * 4 + 12 * rb * sw2 * 4
    return pl.pallas_call(
        functools.partial(_s5_kernel, rb=rb, seg=seg, nseg=nseg, sw=sw, gl=gl, s_=s_, p_=p_),
        grid=(nt, nblocks),
        in_specs=in_specs,
        out_specs=[pl.BlockSpec((rows, LANE), lambda t, r: (r, t)),
                   pl.BlockSpec((None, nseg, sw2), lambda t, r: (t, 0, 0))],
        out_shape=[jax.ShapeDtypeStruct((m, width), BF16),
                   jax.ShapeDtypeStruct((nt, nseg, sw2), F32)],
        scratch_shapes=[pltpu.VMEM((nseg, sw2), F32), pltpu.VMEM((rows, LANE), F32),
                        pltpu.VMEM((tl, sw2), BF16), pltpu.VMEM((tl, tl), BF16), pltpu.VMEM((sw2, tl), BF16)],
        compiler_params=_params(("parallel", "arbitrary"), vmem),
        name=name,
    )(vg, um, km, cm, ptab, d_row, h0)


def _s5_tables(lam_re, lam_im, log_step, b_re, b_im, c_re, c_im):
    t_ = S5_T
    g_, p = lam_re.shape
    s_ = b_re.shape[-1]
    gl = LANE // s_
    nt = g_ // gl
    lam = lax.complex(lam_re.astype(F32), lam_im.astype(F32))
    delta = jnp.exp(log_step.astype(F32))[:, None]
    lam_bar = jnp.exp(lam * delta)
    b_bar = ((lam_bar - 1.0) / lam)[..., None] * lax.complex(b_re.astype(F32), b_im.astype(F32))
    c_t = lax.complex(c_re.astype(F32), c_im.astype(F32))

    def power(d):
        d = jnp.asarray(d, F32)[..., None, None]
        return jnp.exp((lam * delta) * d)

    pw = power(jnp.arange(t_ + 1))
    ub = pw[t_ - 1 - jnp.arange(t_)][:, :, :, None] * b_bar[None]
    ub = ub.transpose(1, 0, 3, 2).reshape(nt, gl, t_, s_, p).transpose(0, 2, 1, 3, 4)
    um = jnp.stack([ub.real, ub.imag], axis=4).astype(BF16).reshape(nt, t_ * gl * s_, 2 * p)

    kd = jnp.einsum('gop,dgp,gpi->dgio', c_t, pw[:t_], b_bar).real
    ii = jnp.arange(t_)[:, None]
    jj = jnp.arange(t_)[None, :]
    kf = jnp.where((jj >= ii)[:, :, None, None, None], kd[jnp.clip(jj - ii, 0, t_ - 1)], 0.0)
    kf = kf.transpose(2, 0, 3, 1, 4).reshape(nt, gl, t_, s_, t_, s_).transpose(0, 2, 1, 3, 4, 5)
    km = kf.astype(BF16).reshape(nt, t_ * gl * s_, t_ * s_)

    cw = c_t[None] * pw[1:][:, :, None, :]
    cw = cw.transpose(1, 3, 0, 2).reshape(nt, gl, p, t_, s_)
    cm = jnp.stack([cw.real, -cw.imag], axis=1).astype(BF16).reshape(nt, 2 * gl * p, t_ * s_)

    nrow = 2 * SUBLANE
    pt = power(t_ * jnp.arange(nrow))
    pt = pt.reshape(nrow, nt, gl * p).transpose(1, 0, 2)
    ptab = jnp.concatenate([pt.real, pt.imag], axis=-1)
    return (um, km, cm), ptab


def _s5_layer(u, lay, w_in, lam_re, lam_im, log_step, b_re, b_im, c_re, c_im, d_skip, w_glu, b_glu, w_out,
              state):
    m = u.shape[0]
    lp, nb, ls = lay['lp'], lay['nb'], lay['ls']
    width = w_glu.shape[0]
    g_, p = lam_re.shape
    s_ = b_re.shape[-1]
    gl = LANE // s_
    nt = g_ // gl
    sw = gl * p
    assert g_ * s_ == width and LANE % s_ == 0 and ls % S5_T == 0 and lp % S5_T == 0

    vg = _mm(u, w_in, n_cols=2 * width, out_dtype=F32, name="s5_in")
    rb_p = _tile(lp // S5_T, 384)
    seg_s = ls // S5_T
    mats, ptab = _s5_tables(lam_re, lam_im, log_step, b_re, b_im, c_re, c_im)
    d_row = d_skip.astype(F32).reshape(1, width)

    def to_tiles(st):
        b = st.shape[0]
        return st.astype(F32).reshape(b, nt, gl, p, 2).transpose(1, 0, 4, 2, 3).reshape(nt, b, 2 * sw)

    def from_tiles(h):
        b = h.shape[1]
        return h.reshape(nt, b, 2, gl, p).transpose(1, 0, 3, 4, 2).reshape(b, g_, p, 2)

    y, hp = _s5_scan(vg, mats, ptab, d_row, jnp.zeros((nt, 1, 2 * sw), F32), rb=rb_p, seg=rb_p, nseg=1,
                     nblocks=lp // (rb_p * S5_T), m=m, name="s5_scan_prompt")
    y_s, hs = _s5_scan(vg[lp:], mats, ptab, d_row, to_tiles(state), rb=nb * seg_s, seg=seg_s, nseg=nb,
                       nblocks=1, m=nb * ls, name="s5_scan_sample")
    y = lax.dynamic_update_slice(y, y_s, (lp, 0))

    def glu(acc, yv, gate, bias):
        return yv.astype(F32) * _sigmoid(acc + bias) * _silu(gate)

    t = _mm(y, w_glu, n_cols=width, out_dtype=BF16, epilogue=glu,
            extras=((y, 'tile', 0), (vg, 'tile', width), (b_glu.reshape(1, width), 'row', 0)), name="s5_glu")
    o = _mm(t, w_out, n_cols=w_out.shape[1], out_dtype=F32, name="s5_out")
    return o, from_tiles(hp), from_tiles(hs)


def _rope_tile(t, cos_ref, sina_ref, sinb_ref, half):
    return (t * cos_ref[...] + pltpu.roll(t, LANE - half, axis=1) * sina_ref[...]
            + pltpu.roll(t, half, axis=1) * sinb_ref[...])


def _mla_prep_kernel(c_ref, qw_ref, kvw_ref, cos_ref, sina_ref, sinb_ref,
                     cqn_ref, ckv_ref, kr_ref, ckr_ref, *, qr, kvr, rope):
    c = c_ref[...]
    cqn_ref[...] = _rms(c[:, :qr], qw_ref[...]).astype(cqn_ref.dtype)
    ckv = _rms(c[:, qr:qr + kvr], kvw_ref[...])
    ckv_ref[...] = ckv
    rot = _rope_tile(c[:, qr + kvr:qr + kvr + LANE], cos_ref, sina_ref, sinb_ref, rope // 2)
    kr_ref[...] = rot[:, :rope]
    ckr_ref[...] = jnp.concatenate([ckv, rot], axis=1).astype(ckr_ref.dtype)


def _mla_prep(gc, col_off, wc, q_norm, kv_norm, tabs, *, qr, kvr, rope):
    m = gc.shape[0]
    tr = _tile(m, 256)
    assert col_off % wc == 0 and wc >= qr + kvr + LANE
    cb = col_off // wc
    row = lambda w: pl.BlockSpec((tr, w), lambda i: (i, 0))
    return pl.pallas_call(
        functools.partial(_mla_prep_kernel, qr=qr, kvr=kvr, rope=rope),
        grid=(m // tr,),
        in_specs=[pl.BlockSpec((tr, wc), lambda i: (i, cb)),
                  pl.BlockSpec((1, qr), lambda i: (0, 0)), pl.BlockSpec((1, kvr), lambda i: (0, 0)),
                  row(LANE), row(LANE), row(LANE)],
        out_specs=[row(qr), row(kvr), row(rope), row(kvr + LANE)],
        out_shape=[jax.ShapeDtypeStruct((m, qr), BF16), jax.ShapeDtypeStruct((m, kvr), F32),
                   jax.ShapeDtypeStruct((m, rope), F32), jax.ShapeDtypeStruct((m, kvr + LANE), BF16)],
        compiler_params=_params(("parallel",), 2 * tr * (wc * 4 + 3 * LANE * 4 + qr * 2 + kvr * 4 + (kvr + LANE) * 2)),
        name="mla_prep",
    )(gc, q_norm.reshape(1, qr), kv_norm.reshape(1, kvr), *tabs)


def _attn_tile(qs, k, v, m_i, l_i, acc, mask):
    s = lax.dot_general(qs, k, (((1,), (1,)), ((), ())), preferred_element_type=F32)
    if mask is not None:
        s = jnp.where(mask, s, NEG)
    m_new = jnp.maximum(m_i, jnp.max(s, axis=1, keepdims=True))
    alpha = jnp.exp(m_i - m_new)
    pexp = jnp.exp(s - m_new)
    l_new = alpha * l_i + jnp.sum(pexp, axis=1, keepdims=True)
    acc_new = alpha * acc + _dot(pexp.astype(BF16), v)
    return m_new, l_new, acc_new


def _q_scaled(q_ref, cos_ref, sina_ref, sinb_ref, *, nope, rope, scale):
    qf = q_ref[...].astype(F32)
    qrot = _rope_tile(qf[:, nope:], cos_ref, sina_ref, sinb_ref, rope // 2)
    return (jnp.concatenate([qf[:, :nope], qrot], axis=1) * scale).astype(BF16)


def _attn_prompt_kernel(q_ref, cos_ref, sina_ref, sinb_ref, k_ref, v_ref, g_ref, o_ref,
                        q_scr, s_buf, p_buf, a_buf, m_scr, l_scr, acc_scr, bias_buf,
                        *, tq, tk, nope, rope, dv, scale, n_pad, first):
    i = pl.program_id(1)
    q_scr[...] = _q_scaled(q_ref, cos_ref, sina_ref, sinb_ref, nope=nope, rope=rope,
                           scale=scale * math.log2(math.e))

    def chunk_of(r):
        return jnp.where(r < first, 0, 1 + (r - first) // CHUNK)

    qchunk = chunk_of(i * tq + lax.broadcasted_iota(jnp.int32, (tq, 1), 0))
    col = lax.broadcasted_iota(jnp.int32, (1, tk), 1)
    ja = (i * tq) // tk
    jl = (i * tq + tq - 1) // tk

    def bias_of(j):
        kcol = j * tk + col
        vis = jnp.where(kcol >= n_pad, chunk_of(kcol), jnp.iinfo(jnp.int32).max) <= qchunk
        return jnp.where(vis, 0.0, 2.0 * NEG)

    @pl.when(i == 0)
    def _():
        bias_buf[0] = jnp.zeros((tq, tk), F32)
        bias_buf[4] = jnp.full((tq, tk), 2.0 * NEG, F32)
        s_buf[1] = jnp.zeros((tq, tk), F32)

    @pl.when(i == -(-tk // tq))
    def _():
        bias_buf[1] = bias_of(0)

    bias_buf[2] = bias_of(ja)
    bias_buf[3] = bias_of(jl)
    p_buf[0] = jnp.zeros((tq, tk), BF16)
    a_buf[0] = jnp.ones((tq, LANE), F32)
    m_scr[...] = jnp.full((tq, LANE), NEG, F32)
    l_scr[...] = jnp.zeros((tq, LANE), F32)
    acc_scr[...] = jnp.zeros((tq, dv), F32)
    nl = tk // LANE

    def stage(t, sa, sb):
        ks = pl.multiple_of(jnp.minimum(t, jl) * tk, tk)
        s_buf[sa] = lax.dot_general(q_scr[...], k_ref[pl.ds(ks, tk), :], (((1,), (1,)), ((), ())),
                                    preferred_element_type=F32)
        vs = pl.multiple_of(jnp.clip(t - 2, 0, jl) * tk, tk)
        acc_scr[...] = a_buf[sa] * acc_scr[...] + _dot(p_buf[sa], v_ref[pl.ds(vs, tk), :])
        j = t - 1
        kind = jnp.where((j < 0) | (j > jl), 4,
                         jnp.where(j == jl, 3, jnp.where(j == ja, 2, jnp.where(j == 0, 1, 0))))
        s = s_buf[sb] + bias_buf[kind]
        m_old = m_scr[...]
        m_new = jnp.maximum(m_old, jnp.max(s, axis=1, keepdims=True))
        a_new = jnp.exp2(m_old - m_new)
        pf = jnp.exp2(s - jnp.concatenate([m_new] * nl, axis=1))
        part = pf[:, 0:LANE]
        for c in range(1, nl):
            part = part + pf[:, c * LANE:(c + 1) * LANE]
        l_scr[...] = a_new * l_scr[...] + part
        m_scr[...] = m_new
        a_buf[sb] = a_new
        p_buf[sb] = pf.astype(BF16)

    def four_stages(tt, carry):
        stage(4 * tt, 0, 1)
        stage(4 * tt + 1, 1, 0)
        stage(4 * tt + 2, 0, 1)
        stage(4 * tt + 3, 1, 0)
        return carry

    n_stage = jl + 3
    n4 = n_stage // 4
    lax.fori_loop(0, n4, four_stages, 0)

    @pl.when(n_stage - 4 * n4 >= 1)
    def _():
        stage(4 * n4, 0, 1)
        stage(4 * n4 + 1, 1, 0)

    @pl.when(n_stage - 4 * n4 == 3)
    def _():
        stage(4 * n4 + 2, 0, 1)
        stage(4 * n4 + 3, 1, 0)
    l_i = jnp.sum(l_scr[...], axis=1, keepdims=True)
    o_ref[...] = (acc_scr[...] / l_i * _silu(g_ref[...].astype(F32))).astype(o_ref.dtype)


def _attn_sample_kernel(q_ref, cos_ref, sina_ref, sinb_ref, k_ref, v_ref, g_ref, prev_ref, o_ref,
                        *, nope, rope, dv, scale, n_valid):
    del prev_ref
    qs = _q_scaled(q_ref, cos_ref, sina_ref, sinb_ref, nope=nope, rope=rope, scale=scale)
    lq, lk = qs.shape[0], k_ref.shape[0]
    mask = lax.broadcasted_iota(jnp.int32, (1, lk), 1) < n_valid
    carry = (jnp.full((lq, 1), NEG, F32), jnp.zeros((lq, 1), F32), jnp.zeros((lq, dv), F32))
    _, l_i, acc = _attn_tile(qs, k_ref[...], v_ref[...], *carry, mask)
    o_ref[...] = (acc / l_i * _silu(g_ref[...].astype(F32))).astype(o_ref.dtype)


def _mla_layer(u, lay, w_in, q_norm, w_uq, kv_norm, w_ukv, w_out, cache_ckv, cache_kr, cache_meta_ckv,
               cache_meta_kr):
    m, d = u.shape
    lp, nb, ls, n_pad, first, n_meta = lay['lp'], lay['nb'], lay['ls'], lay['n_pad'], lay['first'], lay['n_meta']
    qr, = q_norm.shape
    kvr, = kv_norm.shape
    rope = cache_kr.shape[-1]
    width = w_out.shape[0]
    heads = (w_uq.shape[1] - w_ukv.shape[1] + width) // rope
    nope = w_uq.shape[1] // heads - rope
    dv = width // heads
    past = cache_ckv.shape[1]
    assert nope == LANE and dv == LANE and 2 * rope == LANE and (qr + kvr) % LANE == 0 and n_meta > 0
    dk = nope + LANE
    scale = float(nope + rope) ** -0.5

    wc = min(c for c in range(qr + kvr + LANE, width + 1, LANE) if width % c == 0)
    w_perm = jnp.concatenate([w_in[:, qr + kvr + rope:], w_in[:, :qr + kvr + rope],
                              jnp.zeros((d, wc - (qr + kvr + rope)), w_in.dtype)], axis=1)
    ncol = width + wc
    gc = _mm(u, w_perm, n_cols=ncol, out_dtype=F32, tn=_tile(math.gcd(width, wc), 512, LANE), name="mla_in")

    pos = np.concatenate([np.maximum(np.arange(lp) - n_pad, 0),
                          np.tile(n_meta + past + np.arange(ls), nb)]).astype(np.float32)
    half = rope // 2
    inv = ROPE_THETA ** (-jnp.arange(half, dtype=F32) / half)
    ang = jnp.asarray(pos)[:, None] * inv[None, :]
    cos, sin, zero = jnp.cos(ang), jnp.sin(ang), jnp.zeros((m, LANE - rope), F32)
    zh = jnp.zeros((m, half), F32)
    tabs = (jnp.concatenate([cos, cos, zero], axis=1), jnp.concatenate([-sin, zh, zero], axis=1),
            jnp.concatenate([zh, sin, zero], axis=1))

    cqn, ckv, kr, ckr = _mla_prep(gc, width, wc, q_norm, kv_norm, tabs, qr=qr, kvr=kvr, rope=rope)

    wq = jnp.pad(w_uq.reshape(qr, heads, nope + rope), ((0, 0), (0, 0), (0, LANE - rope))).reshape(qr, heads * dk)
    wkv = w_ukv.reshape(kvr, heads, nope + dv)
    eye = jnp.pad(jnp.eye(rope, dtype=w_ukv.dtype), ((0, LANE - rope), (0, LANE - rope)))
    wk = jnp.concatenate([
        jnp.pad(wkv[:, :, :nope], ((0, 0), (0, 0), (0, LANE))),
        jnp.pad(jnp.broadcast_to(eye[:, None, :], (LANE, heads, LANE)), ((0, 0), (0, 0), (nope, 0)))],
        axis=0).reshape(kvr + LANE, heads * dk)
    wv = jnp.pad(wkv[:, :, nope:], ((0, LANE), (0, 0), (0, 0))).reshape(kvr + LANE, heads * dv)

    tq = _tile(lp, 384, LANE)
    tk = 4 * LANE
    lkv = -(-lp // tk) * tk
    if not (tq <= tk and lkv <= m):
        tk, lkv = tq, lp
    q = _mm(cqn, wq, n_cols=heads * dk, out_dtype=BF16, name="mla_q")
    k_p = _mm(ckr, wk, n_cols=heads * dk, out_dtype=BF16, rows=lkv, name="mla_k_prompt")
    v_p = _mm(ckr, wv, n_cols=heads * dv, out_dtype=BF16, rows=lkv, name="mla_v_prompt")

    s_chunk = 1 + (past + np.arange(ls)) // CHUNK
    k_chunk = np.concatenate([1 + np.arange(past) // CHUNK, s_chunk, np.zeros(n_meta, np.int64)])
    assert (k_chunk[None, :] <= s_chunk[:, None]).all()
    n_valid = past + ls + n_meta
    lk = -(-n_valid // LANE) * LANE

    def cat_rows(ckv_part, kr_part):
        return jnp.concatenate([ckv_part.astype(BF16), kr_part.astype(BF16),
                                jnp.zeros(ckv_part.shape[:-1] + (LANE - rope,), BF16)], axis=-1)

    ckr_s = jnp.concatenate([
        cat_rows(cache_ckv, cache_kr), ckr[lp:].reshape(nb, ls, kvr + LANE),
        cat_rows(cache_meta_ckv, cache_meta_kr), jnp.zeros((nb, lk - n_valid, kvr + LANE), BF16)],
        axis=1).reshape(nb * lk, kvr + LANE)
    k_s = _mm(ckr_s, wk, n_cols=heads * dk, out_dtype=BF16, name="mla_k_sample")
    v_s = _mm(ckr_s, wv, n_cols=heads * dv, out_dtype=BF16, name="mla_v_sample")

    assert first % CHUNK == 0 and tq % CHUNK == 0 and first <= tq <= tk and lkv >= 2 * tk
    kw = dict(nope=nope, rope=rope, dv=dv, scale=scale)
    qspec = lambda rows, f: [pl.BlockSpec((rows, dk), f), pl.BlockSpec((rows, LANE), lambda *a: (f(*a)[0], 0)),
                             pl.BlockSpec((rows, LANE), lambda *a: (f(*a)[0], 0)),
                             pl.BlockSpec((rows, LANE), lambda *a: (f(*a)[0], 0))]
    o = pl.pallas_call(
        functools.partial(_attn_prompt_kernel, tq=tq, tk=tk, n_pad=n_pad, first=first, **kw),
        grid=(heads, lp // tq),
        in_specs=qspec(tq, lambda h, i: (i, h)) + [
            pl.BlockSpec((lkv, dk), lambda h, i: (0, h)), pl.BlockSpec((lkv, dv), lambda h, i: (0, h)),
            pl.BlockSpec((tq, dv), lambda h, i: (i, h))],
        out_specs=pl.BlockSpec((tq, dv), lambda h, i: (i, h)),
        out_shape=jax.ShapeDtypeStruct((m, width), BF16),
        scratch_shapes=[pltpu.VMEM((tq, dk), BF16), pltpu.VMEM((2, tq, tk), F32), pltpu.VMEM((2, tq, tk), BF16),
                        pltpu.VMEM((2, tq, LANE), F32), pltpu.VMEM((tq, LANE), F32), pltpu.VMEM((tq, LANE), F32),
                        pltpu.VMEM((tq, dv), F32), pltpu.VMEM((5, tq, tk), F32)],
        compiler_params=_params(("parallel", "arbitrary"), 2 * lkv * (dk + dv) * 2 + 20 * tq * tk * 4),
        name="mla_attn_prompt",
    )(q, *tabs, k_p, v_p, gc)
    assert lp % ls == 0
    rb0 = lp // ls
    o = pl.pallas_call(
        functools.partial(_attn_sample_kernel, n_valid=n_valid, **kw),
        grid=(nb, heads),
        in_specs=qspec(ls, lambda b, h: (rb0 + b, h)) + [
            pl.BlockSpec((lk, dk), lambda b, h: (b, h)), pl.BlockSpec((lk, dv), lambda b, h: (b, h)),
            pl.BlockSpec((ls, dv), lambda b, h: (rb0 + b, h)), pl.BlockSpec(memory_space=pl.ANY)],
        out_specs=pl.BlockSpec((ls, dv), lambda b, h: (rb0 + b, h)),
        out_shape=jax.ShapeDtypeStruct((m, width), BF16),
        input_output_aliases={7: 0},
        compiler_params=_params(("parallel", "arbitrary"), 2 * lk * (dk + dv) * 2 + 8 * ls * lk * 4),
        name="mla_attn_sample",
    )(q, *tabs, k_s, v_s, gc, o)
    out = _mm(o, w_out, n_cols=w_out.shape[1], out_dtype=F32, name="mla_out")
    return out, ckv, kr


def kernel(x_prompt, x_sample, state_ssd, state_conv, state_s5, cache_ckv, cache_kr, cache_meta_ckv, cache_meta_kr, meta_tokens, norm_pre, norm_post, ssd_w_in, ssd_conv_w, ssd_conv_b, ssd_dt_bias, ssd_a_log, ssd_d, ssd_norm, ssd_w_out, s5_w_in, s5_lam_re, s5_lam_im, s5_log_step, s5_b_re, s5_b_im, s5_c_re, s5_c_im, s5_d, s5_w_glu, s5_b_glu, s5_w_out, mla_w_in, mla_q_norm, mla_w_uq, mla_kv_norm, mla_w_ukv, mla_w_out):
    bp, seq, d = x_prompt.shape
    nb, ls, _ = x_sample.shape
    n_meta = meta_tokens.shape[0]
    depth = norm_pre.shape[0]
    assert bp == 1 and seq % LANE == 0 and ls == CHUNK
    n_pad = (-n_meta) % LANE
    first = n_pad + n_meta
    lp = first + seq
    m = lp + nb * ls
    lay = dict(lp=lp, nb=nb, ls=ls, n_pad=n_pad, first=first, n_meta=n_meta)

    h = jnp.concatenate([jnp.zeros((n_pad, d), F32), meta_tokens.astype(F32), x_prompt[0].astype(F32),
                         x_sample.reshape(nb * ls, d).astype(F32)], axis=0)
    u = _prenorm(h, norm_pre[0])

    ssd_p, ssd_s, conv_p, conv_s, s5_p, s5_s = [], [], [], [], [], []
    ckv_all, kr_all = [], []
    for i in range(depth):
        kind, j = i % 3, i // 3
        if kind == 0:
            o, sp, ss, cp, cs = _ssd_layer(u, lay, j, ssd_w_in, ssd_conv_w[j], ssd_conv_b[j], ssd_dt_bias[j],
                                           ssd_a_log[j], ssd_d[j], ssd_norm[j], ssd_w_out, state_conv[j],
                                           state_ssd[j])
            ssd_p.append(sp.astype(state_ssd.dtype))
            ssd_s.append(ss.astype(state_ssd.dtype))
            conv_p.append(cp)
            conv_s.append(cs)
        elif kind == 1:
            o, hpp, hss = _s5_layer(u, lay, s5_w_in[j], s5_lam_re[j], s5_lam_im[j], s5_log_step[j], s5_b_re[j],
                                    s5_b_im[j], s5_c_re[j], s5_c_im[j], s5_d[j], s5_w_glu[j], s5_b_glu[j],
                                    s5_w_out[j], state_s5[j])
            s5_p.append(hpp.astype(state_s5.dtype))
            s5_s.append(hss.astype(state_s5.dtype))
        else:
            o, ckv, kr = _mla_layer(u, lay, mla_w_in[j], mla_q_norm[j], mla_w_uq[j], mla_kv_norm[j],
                                    mla_w_ukv[j], mla_w_out[j], cache_ckv[j], cache_kr[j], cache_meta_ckv[j],
                                    cache_meta_kr[j])
            ckv_all.append(ckv)
            kr_all.append(kr)
        if i + 1 < depth:
            h, u = _postnorm(h, o, norm_post[i], norm_pre[i + 1], n_pad)
        else:
            y_prompt = _lastnorm(h, o, norm_post[i], first, lp - first)[None]
            y_sample = _lastnorm(h, o, norm_post[i], lp, nb * ls).reshape(nb, ls, d)

    def split(a):
        return (a[:, None, n_pad:first], a[:, None, first:lp], a[:, lp:].reshape(a.shape[0], nb, ls, a.shape[-1]))

    mckv_p, ckv_p, ckv_s = split(jnp.stack(ckv_all))
    mkr_p, kr_p, kr_s = split(jnp.stack(kr_all))
    return (y_prompt, y_sample, jnp.stack(ssd_p), jnp.stack(ssd_s), jnp.stack(conv_p), jnp.stack(conv_s),
            jnp.stack(s5_p), jnp.stack(s5_s), mckv_p, mkr_p, ckv_p, kr_p, ckv_s, kr_s)
```

```python
import functools
import math

import numpy as np
import jax
import jax.numpy as jnp
from jax import lax
from jax.experimental import pallas as pl
from jax.experimental.pallas import tpu as pltpu

F32 = jnp.float32
BF16 = jnp.bfloat16

EPS = 1e-6
CHUNK = 64
ROPE_THETA = 10000.0

LANE = 128
SUBLANE = 8
VMEM_CAP = 56 * 1024 * 1024
W_BLOCK_BYTES = 8 * 1024 * 1024
NEG = -1e30
S5_T = 8
ROW_TILE = 256
MM_ROWS = 1024
MM_K = 4096
MM_COLS = 2048
SSD_CHUNK = 128
S5_ROWS = 384
ATTN_Q_ROWS = 384
ATTN_K_ROWS = 512


def _tile(dim, target, align=SUBLANE):
    best = None
    for t in range(align, min(dim, target) + 1, align):
        if dim % t == 0:
            best = t
    assert best is not None, (dim, target, align)
    return best


def _params(sem, vmem_bytes):
    limit = int(min(VMEM_CAP, max(vmem_bytes * 5 // 4 + (4 << 20), 16 << 20)))
    return pltpu.CompilerParams(dimension_semantics=sem, vmem_limit_bytes=limit)


def _sigmoid(x):
    return 0.5 + 0.5 * jnp.tanh(0.5 * x)


def _silu(x):
    h = 0.5 * x
    return h + h * jnp.tanh(h)


def _softplus(x):
    return jnp.maximum(x, 0.0) + jnp.log1p(jnp.exp(-jnp.abs(x)))


def _gelu_tanh(x):
    return 0.5 * x * (1.0 + jnp.tanh(math.sqrt(2.0 / math.pi) * (x + 0.044715 * (x * x * x))))


def _split3(x):
    hi = x.astype(BF16)
    r1 = x - hi.astype(F32)
    mid = r1.astype(BF16)
    lo = (r1 - mid.astype(F32)).astype(BF16)
    return hi, mid, lo


def _dot(a, b):
    return jnp.dot(a, b, preferred_element_type=F32)


def _dot_f32_lhs(x, e):
    hi, mid, lo = _split3(x)
    return _dot(hi, e) + _dot(mid, e) + _dot(lo, e)


def _dot_f32_rhs(e, x):
    hi, mid, lo = _split3(x)
    return _dot(e, hi) + _dot(e, mid) + _dot(e, lo)


def _mm_kernel(*refs, nk, n_extra, epilogue):
    x_ref, w_ref = refs[0], refs[1]
    extra = refs[2:2 + n_extra]
    o_ref = refs[2 + n_extra]
    wb_ref = refs[3 + n_extra]
    i = pl.program_id(1)
    k = pl.program_id(2)

    @pl.when(i == 0)
    def _():
        wb_ref[k] = w_ref[...].astype(BF16)

    part = _dot(x_ref[...], wb_ref[k])

    def finish(acc):
        vals = [e[...] for e in extra]
        o_ref[...] = (epilogue(acc, *vals) if epilogue is not None else acc).astype(o_ref.dtype)

    if nk == 1:
        finish(part)
    else:
        acc_ref = refs[4 + n_extra]

        @pl.when(k == 0)
        def _():
            acc_ref[...] = part

        @pl.when(k > 0)
        def _():
            acc_ref[...] += part

        @pl.when(k == nk - 1)
        def _():
            finish(acc_ref[...])


def _mm(x, w, *, n_cols, col_off=0, out_dtype, rows=None, tn=None, tm_target=MM_ROWS, tk_target=MM_K,
        epilogue=None, extras=(), layer=None, name):
    m = x.shape[0] if rows is None else rows
    kdim = x.shape[1]
    assert w.shape[-2] == kdim and x.dtype == BF16 and w.ndim == (2 if layer is None else 3)
    tm = _tile(m, tm_target, 16)
    tk = _tile(kdim, tk_target, LANE)
    if tn is None:
        cap = max(LANE, min(MM_COLS, W_BLOCK_BYTES // (4 * tk)))
        tn = _tile(math.gcd(n_cols, col_off) if col_off else n_cols, cap, LANE)
    assert n_cols % tn == 0 and col_off % tn == 0, (n_cols, col_off, tn)
    nk = kdim // tk
    cb = col_off // tn
    grid = (n_cols // tn, m // tm, nk)

    def w_map(j, i, k):
        return (jnp.where(i == 0, k, nk - 1), j + cb)

    if layer is None:
        w_spec = pl.BlockSpec((tk, tn), w_map)
    else:
        w_spec = pl.BlockSpec((None, tk, tn), lambda j, i, k: (layer,) + w_map(j, i, k))
    in_specs = [pl.BlockSpec((tm, tk), lambda j, i, k: (i, k)), w_spec]
    args = [x, w]
    for arr, kind, off in extras:
        ob = off // tn
        assert off % tn == 0
        if kind == 'tile':
            in_specs.append(pl.BlockSpec((tm, tn), lambda j, i, k, ob=ob: (i, j + ob)))
        else:
            in_specs.append(pl.BlockSpec((1, tn), lambda j, i, k, ob=ob: (0, j + ob)))
        args.append(arr)
    scratch = [pltpu.VMEM((nk, tk, tn), BF16)]
    if nk > 1:
        scratch.append(pltpu.VMEM((tm, tn), F32))
    osz = jnp.dtype(out_dtype).itemsize
    vmem = (2 * tm * tk * 2 + 2 * tk * tn * 4 + nk * tk * tn * 2 + tm * tn * 4 * 2 + 2 * tm * tn * osz
            + sum(2 * tm * tn * a.dtype.itemsize for a, kind, _ in extras if kind == 'tile'))
    return pl.pallas_call(
        functools.partial(_mm_kernel, nk=nk, n_extra=len(extras), epilogue=epilogue),
        grid=grid,
        in_specs=in_specs,
        out_specs=pl.BlockSpec((tm, tn), lambda j, i, k: (i, j)),
        out_shape=jax.ShapeDtypeStruct((m, n_cols), out_dtype),
        scratch_shapes=scratch,
        compiler_params=_params(("parallel", "arbitrary", "arbitrary"), vmem),
        name=name,
    )(*args)


def _rms(x, w):
    return x * lax.rsqrt(jnp.mean(x * x, axis=-1, keepdims=True) + EPS) * w


def _prenorm_kernel(h_ref, w_ref, u_ref):
    u_ref[...] = _rms(h_ref[...], w_ref[...]).astype(u_ref.dtype)


def _prenorm(h, w_pre):
    m, d = h.shape
    tr = _tile(m, ROW_TILE)
    return pl.pallas_call(
        _prenorm_kernel,
        grid=(m // tr,),
        in_specs=[pl.BlockSpec((tr, d), lambda i: (i, 0)), pl.BlockSpec((1, d), lambda i: (0, 0))],
        out_specs=pl.BlockSpec((tr, d), lambda i: (i, 0)),
        out_shape=jax.ShapeDtypeStruct((m, d), BF16),
        compiler_params=_params(("parallel",), 2 * tr * d * 6),
        name="prenorm",
    )(h, w_pre.reshape(1, d))


def _postnorm_kernel(h_ref, o_ref, wpost_ref, wpre_ref, hout_ref, u_ref, *, tr, n_pad):
    rows = pl.program_id(0) * tr + lax.broadcasted_iota(jnp.int32, (tr, 1), 0)
    h = h_ref[...] + _rms(o_ref[...].astype(F32), wpost_ref[...])
    h = jnp.where(rows >= n_pad, h, 0.0)
    hout_ref[...] = h
    u_ref[...] = _rms(h, wpre_ref[...]).astype(u_ref.dtype)


def _postnorm(h, o, w_post, w_pre, n_pad):
    m, d = h.shape
    tr = _tile(m, ROW_TILE)
    row = pl.BlockSpec((tr, d), lambda i: (i, 0))
    vec = pl.BlockSpec((1, d), lambda i: (0, 0))
    return pl.pallas_call(
        functools.partial(_postnorm_kernel, tr=tr, n_pad=n_pad),
        grid=(m // tr,),
        in_specs=[row, row, vec, vec],
        out_specs=[row, row],
        out_shape=[jax.ShapeDtypeStruct((m, d), F32), jax.ShapeDtypeStruct((m, d), BF16)],
        compiler_params=_params(("parallel",), 2 * tr * d * (4 + 4 + 4 + 2)),
        name="postnorm",
    )(h, o, w_post.reshape(1, d), w_pre.reshape(1, d))


def _lastnorm_kernel(h_ref, o_ref, wpost_ref, y_ref):
    y_ref[...] = h_ref[...] + _rms(o_ref[...].astype(F32), wpost_ref[...])


def _lastnorm(h, o, w_post, row0, nrows):
    d = h.shape[1]
    tr = _tile(math.gcd(row0, nrows), ROW_TILE)
    rb0 = row0 // tr
    return pl.pallas_call(
        _lastnorm_kernel,
        grid=(nrows // tr,),
        in_specs=[pl.BlockSpec((tr, d), lambda i: (rb0 + i, 0)), pl.BlockSpec((tr, d), lambda i: (rb0 + i, 0)),
                  pl.BlockSpec((1, d), lambda i: (0, 0))],
        out_specs=pl.BlockSpec((tr, d), lambda i: (i, 0)),
        out_shape=jax.ShapeDtypeStruct((nrows, d), F32),
        compiler_params=_params(("parallel",), 2 * tr * d * 12),
        name="lastnorm",
    )(h, o, w_post.reshape(1, d))


def _ssd_kernel(z_ref, x_ref, b_ref, c_ref, cwx_ref, cwb_ref, cwc_ref, cbx_ref, cbb_ref, cbc_ref,
                bufx_ref, bufb_ref, bufc_ref, dtc_ref, dtr_ref, dtbc_ref, dtbr_ref, ac_ref, ar_ref,
                dsk_ref, nw_ref, h0_ref, o_ref, hout_ref, xext, bext, cext, st, ybuf,
                *, q, hg, p, n_pad, nchunks):
    c = pl.program_id(2)
    gw = hg * p

    @pl.when(c == 0)
    def _():
        xext[0:SUBLANE, :] = bufx_ref[...]
        bext[0:SUBLANE, :] = bufb_ref[...]
        cext[0:SUBLANE, :] = bufc_ref[...]
        st[...] = h0_ref[...].T

    def conv_silu(ext, raw_ref, w_ref, bias_ref):
        ext[SUBLANE:SUBLANE + q, :] = raw_ref[...]
        e = ext[...]
        z = w_ref[0:1, :] * e
        for tap in range(1, 4):
            z = pltpu.roll(z, 1, axis=0) + w_ref[tap:tap + 1, :] * e
        ext[0:SUBLANE, :] = ext[q:q + SUBLANE, :]
        return _silu(z[SUBLANE:, :] + bias_ref[...])

    xs = conv_silu(xext, x_ref, cwx_ref, cbx_ref)
    bc = conv_silu(bext, b_ref, cwb_ref, cbb_ref)
    cc = conv_silu(cext, c_ref, cwc_ref, cbc_ref)

    rows_c = c * q + lax.broadcasted_iota(jnp.int32, (q, 1), 0)
    rows_r = c * q + lax.broadcasted_iota(jnp.int32, (1, q), 1)
    dt_c = jnp.where(rows_c >= n_pad, _softplus(dtc_ref[...] + dtbc_ref[...]), 0.0)
    dt_r = jnp.where(rows_r >= n_pad, _softplus(dtr_ref[...] + dtbr_ref[...]), 0.0)
    ii = lax.broadcasted_iota(jnp.int32, (q, q), 0)
    jj = lax.broadcasted_iota(jnp.int32, (q, q), 1)
    tri = jj <= ii
    tril = jnp.where(tri, 1.0, 0.0).astype(BF16)
    triu = jnp.where(ii <= jj, 1.0, 0.0).astype(BF16)
    acum = _dot_f32_rhs(tril, dt_c * ac_ref[...])
    acum_t = _dot_f32_lhs(dt_r * ar_ref[...], triu)

    eh = lax.broadcasted_iota(jnp.int32, (LANE, gw), 0)
    el = lax.broadcasted_iota(jnp.int32, (LANE, gw), 1)
    expand = jnp.where(el // p == eh, 1.0, 0.0).astype(BF16)
    dt_x = _dot_f32_lhs(dt_c, expand)
    acum_x = _dot_f32_lhs(acum, expand)
    last_x = acum_x[q - 1:q, :]
    decay_in = jnp.exp(acum_x)
    decay_out = jnp.exp(last_x - acum_x)
    decay_all = jnp.exp(last_x)

    xdt = xs * dt_x
    xdt_b = xdt.astype(BF16)
    xw_b = (xdt * decay_out).astype(BF16)
    bc_t = bc.T.astype(BF16)
    cc_b = cc.astype(BF16)
    cb = _dot(cc_b, bc_t)
    lane_head = lax.broadcasted_iota(jnp.int32, (q, LANE), 1) // p
    heads_per_slab = LANE // p

    for s in range(gw // LANE):
        sl = slice(s * LANE, (s + 1) * LANE)
        y = _dot(cc_b, st[:, sl].astype(BF16)) * decay_in[:, sl]
        xd = xdt_b[:, sl]
        for r in range(heads_per_slab):
            hh = s * heads_per_slab + r
            seg = acum[:, hh:hh + 1] - acum_t[hh:hh + 1, :]
            dec = jnp.exp(jnp.where(tri, seg, NEG))
            mh = (cb * dec).astype(BF16)
            y = y + _dot(mh, jnp.where(lane_head == r, xd, jnp.zeros_like(xd)))
        st[:, sl] = st[:, sl] * decay_all[:, sl] + _dot(bc_t, xw_b[:, sl])
        ybuf[:, sl] = y + xs[:, sl] * dsk_ref[:, sl]

    g = ybuf[...] * _silu(z_ref[...])
    g = g * lax.rsqrt(jnp.mean(g * g, axis=-1, keepdims=True) + EPS)
    o_ref[...] = (g * nw_ref[...]).astype(o_ref.dtype)

    @pl.when(c == nchunks - 1)
    def _():
        hout_ref[...] = st[...].T


def _ssd_scan(zx, dt_c, dt_r, prm, conv_buf8, h0, *, q, nseq, nchunks, row0, n_pad, out_prev, m, name):
    g_, hg, p, n, inner = prm['G'], prm['hg'], prm['P'], prm['N'], prm['inner']
    gw = hg * p
    hgp = dt_r.shape[2]
    assert row0 % q == 0
    rb0 = row0 // q
    xo, bo, co = inner // gw, (2 * inner) // n, (2 * inner + g_ * n) // n
    cbo, cco = inner // n, (inner + g_ * n) // n

    def rmap(s, g, c):
        return rb0 + s * nchunks + c

    in_specs = [
        pl.BlockSpec((q, gw), lambda s, g, c: (rmap(s, g, c), g)),
        pl.BlockSpec((q, gw), lambda s, g, c: (rmap(s, g, c), xo + g)),
        pl.BlockSpec((q, n), lambda s, g, c: (rmap(s, g, c), bo + g)),
        pl.BlockSpec((q, n), lambda s, g, c: (rmap(s, g, c), co + g)),
        pl.BlockSpec((4, gw), lambda s, g, c: (0, g)),
        pl.BlockSpec((4, n), lambda s, g, c: (0, cbo + g)),
        pl.BlockSpec((4, n), lambda s, g, c: (0, cco + g)),
        pl.BlockSpec((1, gw), lambda s, g, c: (0, g)),
        pl.BlockSpec((1, n), lambda s, g, c: (0, cbo + g)),
        pl.BlockSpec((1, n), lambda s, g, c: (0, cco + g)),
        pl.BlockSpec((None, SUBLANE, gw), lambda s, g, c: (s, 0, g)),
        pl.BlockSpec((None, SUBLANE, n), lambda s, g, c: (s, 0, cbo + g)),
        pl.BlockSpec((None, SUBLANE, n), lambda s, g, c: (s, 0, cco + g)),
        pl.BlockSpec((None, q, LANE), lambda s, g, c: (g, rmap(s, g, c), 0)),
        pl.BlockSpec((None, None, hgp, q), lambda s, g, c: (g, s * nchunks + c, 0, 0)),
        pl.BlockSpec((None, 1, LANE), lambda s, g, c: (g, 0, 0)),
        pl.BlockSpec((None, hgp, 1), lambda s, g, c: (g, 0, 0)),
        pl.BlockSpec((None, 1, LANE), lambda s, g, c: (g, 0, 0)),
        pl.BlockSpec((None, hgp, 1), lambda s, g, c: (g, 0, 0)),
        pl.BlockSpec((1, gw), lambda s, g, c: (0, g)),
        pl.BlockSpec((1, gw), lambda s, g, c: (0, g)),
        pl.BlockSpec((None, None, gw, n), lambda s, g, c: (s, g, 0, 0)),
    ]
    args = [zx, zx, zx, zx, prm['conv_w'], prm['conv_w'], prm['conv_w'], prm['conv_b'], prm['conv_b'],
            prm['conv_b'], conv_buf8, conv_buf8, conv_buf8, dt_c, dt_r, prm['dtb_c'], prm['dtb_r'],
            prm['a_c'], prm['a_r'], prm['d_x'], prm['norm_w'], h0]
    aliases = {}
    if out_prev is not None:
        in_specs.append(pl.BlockSpec(memory_space=pl.ANY))
        args.append(out_prev)
        aliases = {len(args) - 1: 0}

    def body(*refs):
        if out_prev is not None:
            refs = refs[:22] + refs[23:]
        _ssd_kernel(*refs, q=q, hg=hg, p=p, n_pad=n_pad, nchunks=nchunks)

    vmem = (2 * (2 * q * gw * 4 + 2 * q * n * 4 + q * gw * 2 + 2 * n * gw * 4) + (2 * q + 8) * gw * 4
            + n * gw * 4 + 24 * q * gw * 4)
    return pl.pallas_call(
        body,
        grid=(nseq, g_, nchunks),
        in_specs=in_specs,
        out_specs=[pl.BlockSpec((q, gw), lambda s, g, c: (rmap(s, g, c), g)),
                   pl.BlockSpec((None, None, gw, n), lambda s, g, c: (s, g, 0, 0))],
        out_shape=[jax.ShapeDtypeStruct((m, inner), BF16),
                   jax.ShapeDtypeStruct((nseq, g_, gw, n), F32)],
        scratch_shapes=[pltpu.VMEM((q + SUBLANE, gw), F32), pltpu.VMEM((q + SUBLANE, n), F32),
                        pltpu.VMEM((q + SUBLANE, n), F32), pltpu.VMEM((n, gw), F32),
                        pltpu.VMEM((q, gw), F32)],
        input_output_aliases=aliases,
        compiler_params=_params(("parallel", "parallel", "arbitrary"), vmem),
        name=name,
    )(*args)


def _ssd_layer(u, lay, j, w_in, conv_w, conv_b, dt_bias, a_log, d_skip, norm_w, w_out, conv_state, ssd_state):
    m = u.shape[0]
    lp, nb, ls, n_pad = lay['lp'], lay['nb'], lay['ls'], lay['n_pad']
    h_, = dt_bias.shape
    inner = w_out.shape[1]
    p = inner // h_
    n = ssd_state.shape[-1]
    conv_dim = conv_w.shape[1]
    g_ = (conv_dim - inner) // (2 * n)
    hg = h_ // g_
    gw = hg * p
    assert LANE % p == 0 and gw % LANE == 0 and n % LANE == 0 and hg <= LANE
    hgp = -(-hg // SUBLANE) * SUBLANE

    zx = _mm(u, w_in, layer=j, n_cols=2 * inner + 2 * g_ * n, out_dtype=F32, name="ssd_in")
    dt_raw = _mm(u, w_in, layer=j, n_cols=h_, col_off=inner + conv_dim, out_dtype=F32, tn=min(h_, LANE),
                 name="ssd_dt")
    dt_c = jnp.pad(dt_raw.reshape(m, g_, hg).transpose(1, 0, 2), ((0, 0), (0, 0), (0, LANE - hg)))

    def dt_rows(q, start, stop):
        return dt_c[:, start:stop, :hgp].reshape(g_, (stop - start) // q, q, hgp).transpose(0, 1, 3, 2)

    a = -jnp.exp(a_log.astype(F32)).reshape(g_, hg)
    dtb = dt_bias.astype(F32).reshape(g_, hg)
    prm = dict(
        G=g_, hg=hg, P=p, N=n, inner=inner,
        conv_w=conv_w, conv_b=conv_b.reshape(1, conv_dim),
        dtb_c=jnp.pad(dtb, ((0, 0), (0, LANE - hg)))[:, None, :],
        dtb_r=jnp.pad(dtb, ((0, 0), (0, hgp - hg)))[:, :, None],
        a_c=jnp.pad(a, ((0, 0), (0, LANE - hg)))[:, None, :],
        a_r=jnp.pad(a, ((0, 0), (0, hgp - hg)))[:, :, None],
        d_x=jnp.repeat(d_skip.astype(F32), p).reshape(1, inner),
        norm_w=norm_w.reshape(1, inner),
    )

    def buf8(buf):
        return jnp.pad(buf.astype(F32), ((0, 0), (SUBLANE - 3, 0), (0, 0)))

    qp = SSD_CHUNK
    assert lp % qp == 0
    y, hp = _ssd_scan(zx, dt_c, dt_rows(qp, 0, lp), prm, jnp.zeros((1, SUBLANE, conv_dim), F32),
                      jnp.zeros((1, g_, gw, n), F32), q=qp, nseq=1, nchunks=lp // qp, row0=0,
                      n_pad=n_pad, out_prev=None, m=m, name="ssd_scan_prompt")
    y, hs = _ssd_scan(zx, dt_c, dt_rows(ls, lp, m), prm, buf8(conv_state),
                      ssd_state.astype(F32).reshape(nb, g_, gw, n), q=ls, nseq=nb,
                      nchunks=1, row0=lp, n_pad=0, out_prev=y, m=m, name="ssd_scan_sample")
    o = _mm(y, w_out, layer=j, n_cols=w_out.shape[2], out_dtype=F32, name="ssd_out")

    xbc = zx[:, inner:]
    conv_p = xbc[lp - 3:lp][None]
    conv_s = xbc[lp:].reshape(nb, ls, conv_dim)[:, ls - 3:]
    return o, hp.reshape(1, h_, p, n), hs.reshape(nb, h_, p, n), conv_p, conv_s


def _cmul(pr, pi, xr, xi):
    return pr * xr - pi * xi, pr * xi + pi * xr


def _s5_kernel(v_ref, um_ref, km_ref, cm_ref, p_ref, d_ref, h0_ref, y_ref, hout_ref, carry, yscr,
               um_s, km_s, cm_s, *, rb, seg, nseg, sw, gl, s_, p_):
    t_ = S5_T

    @pl.when(pl.program_id(1) == 0)
    def _():
        carry[...] = h0_ref[...]

        def expand(c_ref, dst, row_div, col_div, inner):
            cw = c_ref.shape[1]
            n_out = dst.shape[1]
            r = lax.broadcasted_iota(jnp.int32, (cw, n_out), 0)
            c = lax.broadcasted_iota(jnp.int32, (cw, n_out), 1)
            rep = jnp.where((r // inner == c // (gl * inner)) & (r % inner == c % inner), 1.0, 0.0).astype(BF16)
            full = _dot(c_ref[...], rep)
            rr = lax.broadcasted_iota(jnp.int32, full.shape, 0)
            cc = lax.broadcasted_iota(jnp.int32, full.shape, 1)
            same = (rr // row_div) % gl == (cc // col_div) % gl
            dst[...] = jnp.where(same, full, 0.0).astype(BF16)

        expand(um_ref, um_s, s_, p_, p_)
        expand(km_ref, km_s, s_, s_, s_)
        expand(cm_ref, cm_s, p_, s_, s_)

    vs = [v_ref[pl.ds(i, rb, stride=t_), :] for i in range(t_)]
    vcat = jnp.concatenate(vs, axis=1).astype(BF16)
    x = _dot(vcat, um_s[...])
    rowk = lax.broadcasted_iota(jnp.int32, (rb, 1), 0) % seg
    hin = carry[...]
    if nseg == 1:
        hin_rows = jnp.broadcast_to(hin, (rb, 2 * sw))
    else:
        hin_rows = jnp.broadcast_to(hin[:, None, :], (nseg, seg, 2 * sw)).reshape(rb, 2 * sw)

    row8 = lax.broadcasted_iota(jnp.int32, (rb, 1), 0) % SUBLANE
    s = 1
    while s < SUBLANE:
        sh = jnp.where(row8 >= s, pltpu.roll(x, s, axis=0), 0.0)
        ar, ai = _cmul(p_ref[s:s + 1, :sw], p_ref[s:s + 1, sw:], sh[:, :sw], sh[:, sw:])
        x = x + jnp.concatenate([ar, ai], axis=1)
        s *= 2
    p8r, p8i = p_ref[1:SUBLANE + 1, :sw], p_ref[1:SUBLANE + 1, sw:]
    if seg == SUBLANE:
        ar, ai = _cmul(jnp.concatenate([p8r] * nseg, axis=0), jnp.concatenate([p8i] * nseg, axis=0),
                       hin_rows[:, :sw], hin_rows[:, sw:])
        x = x + jnp.concatenate([ar, ai], axis=1)
    else:
        groups = []
        prev = hin_rows[0:SUBLANE, :]
        for gi in range(seg // SUBLANE):
            ar, ai = _cmul(p8r, p8i, prev[:, :sw], prev[:, sw:])
            groups.append(x[gi * SUBLANE:(gi + 1) * SUBLANE, :] + jnp.concatenate([ar, ai], axis=1))
            prev = jnp.broadcast_to(groups[-1][SUBLANE - 1:SUBLANE, :], (SUBLANE, 2 * sw))
        x = jnp.concatenate(groups, axis=0)
    hb = jnp.where(rowk == 0, hin_rows, pltpu.roll(x, 1, axis=0))

    y_all = _dot(vcat, km_s[...]) + _dot(hb.astype(BF16), cm_s[...])
    for j in range(t_):
        yj = y_all[:, j * LANE:(j + 1) * LANE] + vs[j] * d_ref[...]
        yscr[pl.ds(j, rb, stride=t_), :] = _gelu_tanh(yj)
    y_ref[...] = yscr[...].astype(y_ref.dtype)

    if nseg == 1:
        xlast = x[seg - 1:seg, :]
    else:
        pick = (lax.broadcasted_iota(jnp.int32, (nseg, rb), 1)
                == lax.broadcasted_iota(jnp.int32, (nseg, rb), 0) * seg + (seg - 1))
        xlast = _dot_f32_rhs(jnp.where(pick, 1.0, 0.0).astype(BF16), x)
    carry[...] = xlast
    hout_ref[...] = xlast


def _s5_scan(vg, mats, ptab, d_row, h0, *, rb, seg, nseg, nblocks, m, name):
    um, km, cm = mats
    nt, tl, p2 = um.shape
    p_ = p2 // 2
    s_ = km.shape[2] // S5_T
    gl = LANE // s_
    sw = gl * p_
    sw2 = 2 * sw
    width = nt * LANE
    rows = rb * S5_T
    assert seg % SUBLANE == 0 and (nseg == 1 or seg == SUBLANE) and rb == nseg * seg
    in_specs = [
        pl.BlockSpec((rows, LANE), lambda t, r: (r, t)),
        pl.BlockSpec((None, tl, p2), lambda t, r: (t, 0, 0)),
        pl.BlockSpec((None, tl, km.shape[2]), lambda t, r: (t, 0, 0)),
        pl.BlockSpec((None, sw2, cm.shape[2]), lambda t, r: (t, 0, 0)),
        pl.BlockSpec((None, ptab.shape[1], sw2), lambda t, r: (t, 0, 0)),
        pl.BlockSpec((1, LANE), lambda t, r: (0, t)),
        pl.BlockSpec((None, nseg, sw2), lambda t, r: (t, 0, 0)),
    ]
    vmem = 2 * rows * LANE * 6 + (tl * sw2 + tl * tl + sw2 * tl) * 2 + 3 * tl * sw2 * 4 + 12 * rb * sw2 * 4
    return pl.pallas_call(
        functools.partial(_s5_kernel, rb=rb, seg=seg, nseg=nseg, sw=sw, gl=gl, s_=s_, p_=p_),
        grid=(nt, nblocks),
        in_specs=in_specs,
        out_specs=[pl.BlockSpec((rows, LANE), lambda t, r: (r, t)),
                   pl.BlockSpec((None, nseg, sw2), lambda t, r: (t, 0, 0))],
        out_shape=[jax.ShapeDtypeStruct((m, width), BF16),
                   jax.ShapeDtypeStruct((nt, nseg, sw2), F32)],
        scratch_shapes=[pltpu.VMEM((nseg, sw2), F32), pltpu.VMEM((rows, LANE), F32),
                        pltpu.VMEM((tl, sw2), BF16), pltpu.VMEM((tl, tl), BF16), pltpu.VMEM((sw2, tl), BF16)],
        compiler_params=_params(("parallel", "arbitrary"), vmem),
        name=name,
    )(vg, um, km, cm, ptab, d_row, h0)


def _s5_tables(lam_re, lam_im, log_step, b_re, b_im, c_re, c_im):
    t_ = S5_T
    g_, p = lam_re.shape
    s_ = b_re.shape[-1]
    gl = LANE // s_
    nt = g_ // gl
    lam = lax.complex(lam_re.astype(F32), lam_im.astype(F32))
    delta = jnp.exp(log_step.astype(F32))[:, None]
    lam_bar = jnp.exp(lam * delta)
    b_bar = ((lam_bar - 1.0) / lam)[..., None] * lax.complex(b_re.astype(F32), b_im.astype(F32))
    c_t = lax.complex(c_re.astype(F32), c_im.astype(F32))

    def power(d):
        d = jnp.asarray(d, F32)[..., None, None]
        return jnp.exp((lam * delta) * d)

    pw = power(jnp.arange(t_ + 1))
    ub = pw[t_ - 1 - jnp.arange(t_)][:, :, :, None] * b_bar[None]
    ub = ub.transpose(1, 0, 3, 2).reshape(nt, gl, t_, s_, p).transpose(0, 2, 1, 3, 4)
    um = jnp.stack([ub.real, ub.imag], axis=4).astype(BF16).reshape(nt, t_ * gl * s_, 2 * p)

    kd = jnp.einsum('gop,dgp,gpi->dgio', c_t, pw[:t_], b_bar).real
    ii = jnp.arange(t_)[:, None]
    jj = jnp.arange(t_)[None, :]
    kf = jnp.where((jj >= ii)[:, :, None, None, None], kd[jnp.clip(jj - ii, 0, t_ - 1)], 0.0)
    kf = kf.transpose(2, 0, 3, 1, 4).reshape(nt, gl, t_, s_, t_, s_).transpose(0, 2, 1, 3, 4, 5)
    km = kf.astype(BF16).reshape(nt, t_ * gl * s_, t_ * s_)

    cw = c_t[None] * pw[1:][:, :, None, :]
    cw = cw.transpose(1, 3, 0, 2).reshape(nt, gl, p, t_, s_)
    cm = jnp.stack([cw.real, -cw.imag], axis=1).astype(BF16).reshape(nt, 2 * gl * p, t_ * s_)

    nrow = 2 * SUBLANE
    pt = power(t_ * jnp.arange(nrow))
    pt = pt.reshape(nrow, nt, gl * p).transpose(1, 0, 2)
    ptab = jnp.concatenate([pt.real, pt.imag], axis=-1)
    return (um, km, cm), ptab


def _s5_layer(u, lay, w_in, lam_re, lam_im, log_step, b_re, b_im, c_re, c_im, d_skip, w_glu, b_glu, w_out,
              state):
    m = u.shape[0]
    lp, nb, ls = lay['lp'], lay['nb'], lay['ls']
    width = w_glu.shape[0]
    g_, p = lam_re.shape
    s_ = b_re.shape[-1]
    gl = LANE // s_
    nt = g_ // gl
    sw = gl * p
    assert g_ * s_ == width and LANE % s_ == 0 and ls % S5_T == 0 and lp % S5_T == 0

    vg = _mm(u, w_in, n_cols=2 * width, out_dtype=F32, name="s5_in")
    rb_p = _tile(lp // S5_T, S5_ROWS)
    seg_s = ls // S5_T
    mats, ptab = _s5_tables(lam_re, lam_im, log_step, b_re, b_im, c_re, c_im)
    d_row = d_skip.astype(F32).reshape(1, width)

    def to_tiles(st):
        b = st.shape[0]
        return st.astype(F32).reshape(b, nt, gl, p, 2).transpose(1, 0, 4, 2, 3).reshape(nt, b, 2 * sw)

    def from_tiles(h):
        b = h.shape[1]
        return h.reshape(nt, b, 2, gl, p).transpose(1, 0, 3, 4, 2).reshape(b, g_, p, 2)

    y, hp = _s5_scan(vg, mats, ptab, d_row, jnp.zeros((nt, 1, 2 * sw), F32), rb=rb_p, seg=rb_p, nseg=1,
                     nblocks=lp // (rb_p * S5_T), m=m, name="s5_scan_prompt")
    y_s, hs = _s5_scan(vg[lp:], mats, ptab, d_row, to_tiles(state), rb=nb * seg_s, seg=seg_s, nseg=nb,
                       nblocks=1, m=nb * ls, name="s5_scan_sample")
    y = lax.dynamic_update_slice(y, y_s, (lp, 0))

    def glu(acc, yv, gate, bias):
        return yv.astype(F32) * _sigmoid(acc + bias) * _silu(gate)

    t = _mm(y, w_glu, n_cols=width, out_dtype=BF16, epilogue=glu,
            extras=((y, 'tile', 0), (vg, 'tile', width), (b_glu.reshape(1, width), 'row', 0)), name="s5_glu")
    o = _mm(t, w_out, n_cols=w_out.shape[1], out_dtype=F32, name="s5_out")
    return o, from_tiles(hp), from_tiles(hs)


def _rope_tile(t, cos_ref, sina_ref, sinb_ref, half):
    return (t * cos_ref[...] + pltpu.roll(t, LANE - half, axis=1) * sina_ref[...]
            + pltpu.roll(t, half, axis=1) * sinb_ref[...])


def _mla_prep_kernel(c_ref, qw_ref, kvw_ref, cos_ref, sina_ref, sinb_ref,
                     cqn_ref, ckv_ref, kr_ref, ckr_ref, *, qr, kvr, rope):
    c = c_ref[...]
    cqn_ref[...] = _rms(c[:, :qr], qw_ref[...]).astype(cqn_ref.dtype)
    ckv = _rms(c[:, qr:qr + kvr], kvw_ref[...])
    ckv_ref[...] = ckv
    rot = _rope_tile(c[:, qr + kvr:qr + kvr + LANE], cos_ref, sina_ref, sinb_ref, rope // 2)
    kr_ref[...] = rot[:, :rope]
    ckr_ref[...] = jnp.concatenate([ckv, rot], axis=1).astype(ckr_ref.dtype)


def _mla_prep(gc, col_off, wc, q_norm, kv_norm, tabs, *, qr, kvr, rope):
    m = gc.shape[0]
    tr = _tile(m, ROW_TILE)
    assert col_off % wc == 0 and wc >= qr + kvr + LANE
    cb = col_off // wc
    row = lambda w: pl.BlockSpec((tr, w), lambda i: (i, 0))
    return pl.pallas_call(
        functools.partial(_mla_prep_kernel, qr=qr, kvr=kvr, rope=rope),
        grid=(m // tr,),
        in_specs=[pl.BlockSpec((tr, wc), lambda i: (i, cb)),
                  pl.BlockSpec((1, qr), lambda i: (0, 0)), pl.BlockSpec((1, kvr), lambda i: (0, 0)),
                  row(LANE), row(LANE), row(LANE)],
        out_specs=[row(qr), row(kvr), row(rope), row(kvr + LANE)],
        out_shape=[jax.ShapeDtypeStruct((m, qr), BF16), jax.ShapeDtypeStruct((m, kvr), F32),
                   jax.ShapeDtypeStruct((m, rope), F32), jax.ShapeDtypeStruct((m, kvr + LANE), BF16)],
        compiler_params=_params(("parallel",), 2 * tr * (wc * 4 + 3 * LANE * 4 + qr * 2 + kvr * 4 + (kvr + LANE) * 2)),
        name="mla_prep",
    )(gc, q_norm.reshape(1, qr), kv_norm.reshape(1, kvr), *tabs)


def _attn_tile(qs, k, v, m_i, l_i, acc, mask):
    s = lax.dot_general(qs, k, (((1,), (1,)), ((), ())), preferred_element_type=F32)
    if mask is not None:
        s = jnp.where(mask, s, NEG)
    m_new = jnp.maximum(m_i, jnp.max(s, axis=1, keepdims=True))
    alpha = jnp.exp(m_i - m_new)
    pexp = jnp.exp(s - m_new)
    l_new = alpha * l_i + jnp.sum(pexp, axis=1, keepdims=True)
    acc_new = alpha * acc + _dot(pexp.astype(BF16), v)
    return m_new, l_new, acc_new


def _q_scaled(q_ref, cos_ref, sina_ref, sinb_ref, *, nope, rope, scale):
    qf = q_ref[...].astype(F32)
    qrot = _rope_tile(qf[:, nope:], cos_ref, sina_ref, sinb_ref, rope // 2)
    return (jnp.concatenate([qf[:, :nope], qrot], axis=1) * scale).astype(BF16)


def _attn_prompt_kernel(q_ref, cos_ref, sina_ref, sinb_ref, k_ref, v_ref, g_ref, o_ref,
                        q_scr, s_buf, p_buf, a_buf, m_scr, l_scr, acc_scr, bias_buf,
                        *, tq, tk, nope, rope, dv, scale, n_pad, first):
    i = pl.program_id(1)
    q_scr[...] = _q_scaled(q_ref, cos_ref, sina_ref, sinb_ref, nope=nope, rope=rope,
                           scale=scale * math.log2(math.e))

    def chunk_of(r):
        return jnp.where(r < first, 0, 1 + (r - first) // CHUNK)

    qchunk = chunk_of(i * tq + lax.broadcasted_iota(jnp.int32, (tq, 1), 0))
    col = lax.broadcasted_iota(jnp.int32, (1, tk), 1)
    ja = (i * tq) // tk
    jl = (i * tq + tq - 1) // tk

    def bias_of(j):
        kcol = j * tk + col
        vis = jnp.where(kcol >= n_pad, chunk_of(kcol), jnp.iinfo(jnp.int32).max) <= qchunk
        return jnp.where(vis, 0.0, 2.0 * NEG)

    @pl.when(i == 0)
    def _():
        bias_buf[0] = jnp.zeros((tq, tk), F32)
        bias_buf[4] = jnp.full((tq, tk), 2.0 * NEG, F32)
        s_buf[1] = jnp.zeros((tq, tk), F32)

    @pl.when(i == -(-tk // tq))
    def _():
        bias_buf[1] = bias_of(0)

    bias_buf[2] = bias_of(ja)
    bias_buf[3] = bias_of(jl)
    p_buf[0] = jnp.zeros((tq, tk), BF16)
    a_buf[0] = jnp.ones((tq, LANE), F32)
    m_scr[...] = jnp.full((tq, LANE), NEG, F32)
    l_scr[...] = jnp.zeros((tq, LANE), F32)
    acc_scr[...] = jnp.zeros((tq, dv), F32)
    nl = tk // LANE

    def stage(t, sa, sb):
        ks = pl.multiple_of(jnp.minimum(t, jl) * tk, tk)
        s_buf[sa] = lax.dot_general(q_scr[...], k_ref[pl.ds(ks, tk), :], (((1,), (1,)), ((), ())),
                                    preferred_element_type=F32)
        vs = pl.multiple_of(jnp.clip(t - 2, 0, jl) * tk, tk)
        acc_scr[...] = a_buf[sa] * acc_scr[...] + _dot(p_buf[sa], v_ref[pl.ds(vs, tk), :])
        j = t - 1
        kind = jnp.where((j < 0) | (j > jl), 4,
                         jnp.where(j == jl, 3, jnp.where(j == ja, 2, jnp.where(j == 0, 1, 0))))
        s = s_buf[sb] + bias_buf[kind]
        m_old = m_scr[...]
        m_new = jnp.maximum(m_old, jnp.max(s, axis=1, keepdims=True))
        a_new = jnp.exp2(m_old - m_new)
        pf = jnp.exp2(s - jnp.concatenate([m_new] * nl, axis=1))
        part = pf[:, 0:LANE]
        for c in range(1, nl):
            part = part + pf[:, c * LANE:(c + 1) * LANE]
        l_scr[...] = a_new * l_scr[...] + part
        m_scr[...] = m_new
        a_buf[sb] = a_new
        p_buf[sb] = pf.astype(BF16)

    def four_stages(tt, carry):
        stage(4 * tt, 0, 1)
        stage(4 * tt + 1, 1, 0)
        stage(4 * tt + 2, 0, 1)
        stage(4 * tt + 3, 1, 0)
        return carry

    n_stage = jl + 3
    n4 = n_stage // 4
    lax.fori_loop(0, n4, four_stages, 0)

    @pl.when(n_stage - 4 * n4 >= 1)
    def _():
        stage(4 * n4, 0, 1)
        stage(4 * n4 + 1, 1, 0)

    @pl.when(n_stage - 4 * n4 == 3)
    def _():
        stage(4 * n4 + 2, 0, 1)
        stage(4 * n4 + 3, 1, 0)
    l_i = jnp.sum(l_scr[...], axis=1, keepdims=True)
    o_ref[...] = (acc_scr[...] / l_i * _silu(g_ref[...].astype(F32))).astype(o_ref.dtype)


def _attn_sample_kernel(q_ref, cos_ref, sina_ref, sinb_ref, k_ref, v_ref, g_ref, prev_ref, o_ref,
                        *, nope, rope, dv, scale, n_valid):
    del prev_ref
    qs = _q_scaled(q_ref, cos_ref, sina_ref, sinb_ref, nope=nope, rope=rope, scale=scale)
    lq, lk = qs.shape[0], k_ref.shape[0]
    mask = lax.broadcasted_iota(jnp.int32, (1, lk), 1) < n_valid
    carry = (jnp.full((lq, 1), NEG, F32), jnp.zeros((lq, 1), F32), jnp.zeros((lq, dv), F32))
    _, l_i, acc = _attn_tile(qs, k_ref[...], v_ref[...], *carry, mask)
    o_ref[...] = (acc / l_i * _silu(g_ref[...].astype(F32))).astype(o_ref.dtype)


def _mla_layer(u, lay, w_in, q_norm, w_uq, kv_norm, w_ukv, w_out, cache_ckv, cache_kr, cache_meta_ckv,
               cache_meta_kr):
    m, d = u.shape
    lp, nb, ls, n_pad, first, n_meta = lay['lp'], lay['nb'], lay['ls'], lay['n_pad'], lay['first'], lay['n_meta']
    qr, = q_norm.shape
    kvr, = kv_norm.shape
    rope = cache_kr.shape[-1]
    width = w_out.shape[0]
    heads = (w_uq.shape[1] - w_ukv.shape[1] + width) // rope
    nope = w_uq.shape[1] // heads - rope
    dv = width // heads
    past = cache_ckv.shape[1]
    assert nope == LANE and dv == LANE and 2 * rope == LANE and (qr + kvr) % LANE == 0 and n_meta > 0
    dk = nope + LANE
    scale = float(nope + rope) ** -0.5

    wc = min(c for c in range(qr + kvr + LANE, width + 1, LANE) if width % c == 0)
    w_perm = jnp.concatenate([w_in[:, qr + kvr + rope:], w_in[:, :qr + kvr + rope],
                              jnp.zeros((d, wc - (qr + kvr + rope)), w_in.dtype)], axis=1)
    ncol = width + wc
    gc = _mm(u, w_perm, n_cols=ncol, out_dtype=F32, name="mla_in")

    pos = np.concatenate([np.maximum(np.arange(lp) - n_pad, 0),
                          np.tile(n_meta + past + np.arange(ls), nb)]).astype(np.float32)
    half = rope // 2
    inv = ROPE_THETA ** (-jnp.arange(half, dtype=F32) / half)
    ang = jnp.asarray(pos)[:, None] * inv[None, :]
    cos, sin, zero = jnp.cos(ang), jnp.sin(ang), jnp.zeros((m, LANE - rope), F32)
    zh = jnp.zeros((m, half), F32)
    tabs = (jnp.concatenate([cos, cos, zero], axis=1), jnp.concatenate([-sin, zh, zero], axis=1),
            jnp.concatenate([zh, sin, zero], axis=1))

    cqn, ckv, kr, ckr = _mla_prep(gc, width, wc, q_norm, kv_norm, tabs, qr=qr, kvr=kvr, rope=rope)

    wq = jnp.pad(w_uq.reshape(qr, heads, nope + rope), ((0, 0), (0, 0), (0, LANE - rope))).reshape(qr, heads * dk)
    wkv = w_ukv.reshape(kvr, heads, nope + dv)
    eye = jnp.pad(jnp.eye(rope, dtype=w_ukv.dtype), ((0, LANE - rope), (0, LANE - rope)))
    wk = jnp.concatenate([
        jnp.pad(wkv[:, :, :nope], ((0, 0), (0, 0), (0, LANE))),
        jnp.pad(jnp.broadcast_to(eye[:, None, :], (LANE, heads, LANE)), ((0, 0), (0, 0), (nope, 0)))],
        axis=0).reshape(kvr + LANE, heads * dk)
    wv = jnp.pad(wkv[:, :, nope:], ((0, LANE), (0, 0), (0, 0))).reshape(kvr + LANE, heads * dv)

    tq = _tile(lp, ATTN_Q_ROWS, LANE)
    tk = ATTN_K_ROWS
    lkv = -(-lp // tk) * tk
    if not (tq <= tk and lkv <= m):
        tk, lkv = tq, lp
    q = _mm(cqn, wq, n_cols=heads * dk, out_dtype=BF16, name="mla_q")
    k_p = _mm(ckr, wk, n_cols=heads * dk, out_dtype=BF16, rows=lkv, name="mla_k_prompt")
    v_p = _mm(ckr, wv, n_cols=heads * dv, out_dtype=BF16, rows=lkv, name="mla_v_prompt")

    s_chunk = 1 + (past + np.arange(ls)) // CHUNK
    k_chunk = np.concatenate([1 + np.arange(past) // CHUNK, s_chunk, np.zeros(n_meta, np.int64)])
    assert (k_chunk[None, :] <= s_chunk[:, None]).all()
    n_valid = past + ls + n_meta
    lk = -(-n_valid // LANE) * LANE

    def cat_rows(ckv_part, kr_part):
        return jnp.concatenate([ckv_part.astype(BF16), kr_part.astype(BF16),
                                jnp.zeros(ckv_part.shape[:-1] + (LANE - rope,), BF16)], axis=-1)

    ckr_s = jnp.concatenate([
        cat_rows(cache_ckv, cache_kr), ckr[lp:].reshape(nb, ls, kvr + LANE),
        cat_rows(cache_meta_ckv, cache_meta_kr), jnp.zeros((nb, lk - n_valid, kvr + LANE), BF16)],
        axis=1).reshape(nb * lk, kvr + LANE)
    k_s = _mm(ckr_s, wk, n_cols=heads * dk, out_dtype=BF16, name="mla_k_sample")
    v_s = _mm(ckr_s, wv, n_cols=heads * dv, out_dtype=BF16, name="mla_v_sample")

    assert first % CHUNK == 0 and tq % CHUNK == 0 and first <= tq <= tk and lkv >= 2 * tk
    kw = dict(nope=nope, rope=rope, dv=dv, scale=scale)
    qspec = lambda rows, f: [pl.BlockSpec((rows, dk), f), pl.BlockSpec((rows, LANE), lambda *a: (f(*a)[0], 0)),
                             pl.BlockSpec((rows, LANE), lambda *a: (f(*a)[0], 0)),
                             pl.BlockSpec((rows, LANE), lambda *a: (f(*a)[0], 0))]
    o = pl.pallas_call(
        functools.partial(_attn_prompt_kernel, tq=tq, tk=tk, n_pad=n_pad, first=first, **kw),
        grid=(heads, lp // tq),
        in_specs=qspec(tq, lambda h, i: (i, h)) + [
            pl.BlockSpec((lkv, dk), lambda h, i: (0, h)), pl.BlockSpec((lkv, dv), lambda h, i: (0, h)),
            pl.BlockSpec((tq, dv), lambda h, i: (i, h))],
        out_specs=pl.BlockSpec((tq, dv), lambda h, i: (i, h)),
        out_shape=jax.ShapeDtypeStruct((m, width), BF16),
        scratch_shapes=[pltpu.VMEM((tq, dk), BF16), pltpu.VMEM((2, tq, tk), F32), pltpu.VMEM((2, tq, tk), BF16),
                        pltpu.VMEM((2, tq, LANE), F32), pltpu.VMEM((tq, LANE), F32), pltpu.VMEM((tq, LANE), F32),
                        pltpu.VMEM((tq, dv), F32), pltpu.VMEM((5, tq, tk), F32)],
        compiler_params=_params(("parallel", "arbitrary"), 2 * lkv * (dk + dv) * 2 + 20 * tq * tk * 4),
        name="mla_attn_prompt",
    )(q, *tabs, k_p, v_p, gc)
    assert lp % ls == 0
    rb0 = lp // ls
    o = pl.pallas_call(
        functools.partial(_attn_sample_kernel, n_valid=n_valid, **kw),
        grid=(nb, heads),
        in_specs=qspec(ls, lambda b, h: (rb0 + b, h)) + [
            pl.BlockSpec((lk, dk), lambda b, h: (b, h)), pl.BlockSpec((lk, dv), lambda b, h: (b, h)),
            pl.BlockSpec((ls, dv), lambda b, h: (rb0 + b, h)), pl.BlockSpec(memory_space=pl.ANY)],
        out_specs=pl.BlockSpec((ls, dv), lambda b, h: (rb0 + b, h)),
        out_shape=jax.ShapeDtypeStruct((m, width), BF16),
        input_output_aliases={7: 0},
        compiler_params=_params(("parallel", "arbitrary"), 2 * lk * (dk + dv) * 2 + 8 * ls * lk * 4),
        name="mla_attn_sample",
    )(q, *tabs, k_s, v_s, gc, o)
    out = _mm(o, w_out, n_cols=w_out.shape[1], out_dtype=F32, name="mla_out")
    return out, ckv, kr


def kernel(x_prompt, x_sample, state_ssd, state_conv, state_s5, cache_ckv, cache_kr, cache_meta_ckv, cache_meta_kr, meta_tokens, norm_pre, norm_post, ssd_w_in, ssd_conv_w, ssd_conv_b, ssd_dt_bias, ssd_a_log, ssd_d, ssd_norm, ssd_w_out, s5_w_in, s5_lam_re, s5_lam_im, s5_log_step, s5_b_re, s5_b_im, s5_c_re, s5_c_im, s5_d, s5_w_glu, s5_b_glu, s5_w_out, mla_w_in, mla_q_norm, mla_w_uq, mla_kv_norm, mla_w_ukv, mla_w_out):
    bp, seq, d = x_prompt.shape
    nb, ls, _ = x_sample.shape
    n_meta = meta_tokens.shape[0]
    depth = norm_pre.shape[0]
    assert bp == 1 and seq % LANE == 0 and ls == CHUNK
    n_pad = (-n_meta) % LANE
    first = n_pad + n_meta
    lp = first + seq
    m = lp + nb * ls
    lay = dict(lp=lp, nb=nb, ls=ls, n_pad=n_pad, first=first, n_meta=n_meta)

    h = jnp.concatenate([jnp.zeros((n_pad, d), F32), meta_tokens.astype(F32), x_prompt[0].astype(F32),
                         x_sample.reshape(nb * ls, d).astype(F32)], axis=0)
    u = _prenorm(h, norm_pre[0])

    ssd_p, ssd_s, conv_p, conv_s, s5_p, s5_s = [], [], [], [], [], []
    ckv_all, kr_all = [], []
    for i in range(depth):
        kind, j = i % 3, i // 3
        if kind == 0:
            o, sp, ss, cp, cs = _ssd_layer(u, lay, j, ssd_w_in, ssd_conv_w[j], ssd_conv_b[j], ssd_dt_bias[j],
                                           ssd_a_log[j], ssd_d[j], ssd_norm[j], ssd_w_out, state_conv[j],
                                           state_ssd[j])
            ssd_p.append(sp.astype(state_ssd.dtype))
            ssd_s.append(ss.astype(state_ssd.dtype))
            conv_p.append(cp)
            conv_s.append(cs)
        elif kind == 1:
            o, hpp, hss = _s5_layer(u, lay, s5_w_in[j], s5_lam_re[j], s5_lam_im[j], s5_log_step[j], s5_b_re[j],
                                    s5_b_im[j], s5_c_re[j], s5_c_im[j], s5_d[j], s5_w_glu[j], s5_b_glu[j],
                                    s5_w_out[j], state_s5[j])
            s5_p.append(hpp.astype(state_s5.dtype))
            s5_s.append(hss.astype(state_s5.dtype))
        else:
            o, ckv, kr = _mla_layer(u, lay, mla_w_in[j], mla_q_norm[j], mla_w_uq[j], mla_kv_norm[j],
                                    mla_w_ukv[j], mla_w_out[j], cache_ckv[j], cache_kr[j], cache_meta_ckv[j],
                                    cache_meta_kr[j])
            ckv_all.append(ckv)
            kr_all.append(kr)
        if i + 1 < depth:
            h, u = _postnorm(h, o, norm_post[i], norm_pre[i + 1], n_pad)
        else:
            y_prompt = _lastnorm(h, o, norm_post[i], first, lp - first)[None]
            y_sample = _lastnorm(h, o, norm_post[i], lp, nb * ls).reshape(nb, ls, d)

    def split(a):
        return (a[:, None, n_pad:first], a[:, None, first:lp], a[:, lp:].reshape(a.shape[0], nb, ls, a.shape[-1]))

    mckv_p, ckv_p, ckv_s = split(jnp.stack(ckv_all))
    mkr_p, kr_p, kr_s = split(jnp.stack(kr_all))
    return (y_prompt, y_sample, jnp.stack(ssd_p), jnp.stack(ssd_s), jnp.stack(conv_p), jnp.stack(conv_s),
            jnp.stack(s5_p), jnp.stack(s5_s), mckv_p, mkr_p, ckv_p, kr_p, ckv_s, kr_s)
```

```python
import functools
import math

import numpy as np
import jax
import jax.numpy as jnp
from jax import lax
from jax.experimental import pallas as pl
from jax.experimental.pallas import tpu as pltpu

F32 = jnp.float32
BF16 = jnp.bfloat16

EPS = 1e-6
CHUNK = 64
ROPE_THETA = 10000.0

LANE = 128
SUBLANE = 8
VMEM_CAP = 56 * 1024 * 1024
W_BLOCK_BYTES = 8 * 1024 * 1024
NEG = -1e30
S5_T = 8
ROW_TILE = 256
MM_ROWS = 1024
MM_K = 4096
MM_COLS = 2048
SSD_CHUNK = 128
S5_ROWS = 384
ATTN_Q_ROWS = 384
ATTN_K_ROWS = 512


def _tile(dim, target, align=SUBLANE):
    best = None
    for t in range(align, min(dim, target) + 1, align):
        if dim % t == 0:
            best = t
    assert best is not None, (dim, target, align)
    return best


def _params(sem, vmem_bytes):
    limit = int(min(VMEM_CAP, max(vmem_bytes * 5 // 4 + (4 << 20), 16 << 20)))
    return pltpu.CompilerParams(dimension_semantics=sem, vmem_limit_bytes=limit)


def _sigmoid(x):
    return 0.5 + 0.5 * jnp.tanh(0.5 * x)


def _silu(x):
    h = 0.5 * x
    return h + h * jnp.tanh(h)


def _softplus(x):
    return jnp.maximum(x, 0.0) + jnp.log1p(jnp.exp(-jnp.abs(x)))


def _gelu_tanh(x):
    return 0.5 * x * (1.0 + jnp.tanh(math.sqrt(2.0 / math.pi) * (x + 0.044715 * (x * x * x))))


def _split3(x):
    hi = x.astype(BF16)
    r1 = x - hi.astype(F32)
    mid = r1.astype(BF16)
    lo = (r1 - mid.astype(F32)).astype(BF16)
    return hi, mid, lo


def _dot(a, b):
    return jnp.dot(a, b, preferred_element_type=F32)


def _dot_f32_lhs(x, e):
    hi, mid, lo = _split3(x)
    return _dot(hi, e) + _dot(mid, e) + _dot(lo, e)


def _dot_f32_rhs(e, x):
    hi, mid, lo = _split3(x)
    return _dot(e, hi) + _dot(e, mid) + _dot(e, lo)


def _mm_kernel(*refs, nk, n_extra, epilogue):
    x_ref, w_ref = refs[0], refs[1]
    extra = refs[2:2 + n_extra]
    o_ref = refs[2 + n_extra]
    wb_ref = refs[3 + n_extra]
    i = pl.program_id(1)
    k = pl.program_id(2)

    @pl.when(i == 0)
    def _():
        wb_ref[k] = w_ref[...].astype(BF16)

    part = _dot(x_ref[...], wb_ref[k])

    def finish(acc):
        vals = [e[...] for e in extra]
        o_ref[...] = (epilogue(acc, *vals) if epilogue is not None else acc).astype(o_ref.dtype)

    if nk == 1:
        finish(part)
    else:
        acc_ref = refs[4 + n_extra]

        @pl.when(k == 0)
        def _():
            acc_ref[...] = part

        @pl.when(k > 0)
        def _():
            acc_ref[...] += part

        @pl.when(k == nk - 1)
        def _():
            finish(acc_ref[...])


def _mm(x, w, *, n_cols, col_off=0, out_dtype, rows=None, tn=None, tm_target=MM_ROWS, tk_target=MM_K,
        epilogue=None, extras=(), layer=None, name):
    m = x.shape[0] if rows is None else rows
    kdim = x.shape[1]
    assert w.shape[-2] == kdim and x.dtype == BF16 and w.ndim == (2 if layer is None else 3)
    tm = _tile(m, tm_target, 16)
    tk = _tile(kdim, tk_target, LANE)
    if tn is None:
        cap = max(LANE, min(MM_COLS, W_BLOCK_BYTES // (4 * tk)))
        tn = _tile(math.gcd(n_cols, col_off) if col_off else n_cols, cap, LANE)
    assert n_cols % tn == 0 and col_off % tn == 0, (n_cols, col_off, tn)
    nk = kdim // tk
    cb = col_off // tn
    grid = (n_cols // tn, m // tm, nk)

    def w_map(j, i, k):
        return (jnp.where(i == 0, k, nk - 1), j + cb)

    if layer is None:
        w_spec = pl.BlockSpec((tk, tn), w_map)
    else:
        w_spec = pl.BlockSpec((None, tk, tn), lambda j, i, k: (layer,) + w_map(j, i, k))
    in_specs = [pl.BlockSpec((tm, tk), lambda j, i, k: (i, k)), w_spec]
    args = [x, w]
    for arr, kind, off in extras:
        ob = off // tn
        assert off % tn == 0
        if kind == 'tile':
            in_specs.append(pl.BlockSpec((tm, tn), lambda j, i, k, ob=ob: (i, j + ob)))
        else:
            in_specs.append(pl.BlockSpec((1, tn), lambda j, i, k, ob=ob: (0, j + ob)))
        args.append(arr)
    scratch = [pltpu.VMEM((nk, tk, tn), BF16)]
    if nk > 1:
        scratch.append(pltpu.VMEM((tm, tn), F32))
    osz = jnp.dtype(out_dtype).itemsize
    vmem = (2 * tm * tk * 2 + 2 * tk * tn * 4 + nk * tk * tn * 2 + tm * tn * 4 * 2 + 2 * tm * tn * osz
            + sum(2 * tm * tn * a.dtype.itemsize for a, kind, _ in extras if kind == 'tile'))
    return pl.pallas_call(
        functools.partial(_mm_kernel, nk=nk, n_extra=len(extras), epilogue=epilogue),
        grid=grid,
        in_specs=in_specs,
        out_specs=pl.BlockSpec((tm, tn), lambda j, i, k: (i, j)),
        out_shape=jax.ShapeDtypeStruct((m, n_cols), out_dtype),
        scratch_shapes=scratch,
        compiler_params=_params(("parallel", "arbitrary", "arbitrary"), vmem),
        name=name,
    )(*args)


def _rms(x, w):
    return x * lax.rsqrt(jnp.mean(x * x, axis=-1, keepdims=True) + EPS) * w


def _prenorm_kernel(h_ref, w_ref, u_ref):
    u_ref[...] = _rms(h_ref[...], w_ref[...]).astype(u_ref.dtype)


def _prenorm(h, w_pre):
    m, d = h.shape
    tr = _tile(m, ROW_TILE)
    return pl.pallas_call(
        _prenorm_kernel,
        grid=(m // tr,),
        in_specs=[pl.BlockSpec((tr, d), lambda i: (i, 0)), pl.BlockSpec((1, d), lambda i: (0, 0))],
        out_specs=pl.BlockSpec((tr, d), lambda i: (i, 0)),
        out_shape=jax.ShapeDtypeStruct((m, d), BF16),
        compiler_params=_params(("parallel",), 2 * tr * d * 6),
        name="prenorm",
    )(h, w_pre.reshape(1, d))


def _postnorm_kernel(h_ref, o_ref, wpost_ref, wpre_ref, hout_ref, u_ref, *, tr, n_pad):
    rows = pl.program_id(0) * tr + lax.broadcasted_iota(jnp.int32, (tr, 1), 0)
    h = h_ref[...] + _rms(o_ref[...].astype(F32), wpost_ref[...])
    h = jnp.where(rows >= n_pad, h, 0.0)
    hout_ref[...] = h
    u_ref[...] = _rms(h, wpre_ref[...]).astype(u_ref.dtype)


def _postnorm(h, o, w_post, w_pre, n_pad):
    m, d = h.shape
    tr = _tile(m, ROW_TILE)
    row = pl.BlockSpec((tr, d), lambda i: (i, 0))
    vec = pl.BlockSpec((1, d), lambda i: (0, 0))
    return pl.pallas_call(
        functools.partial(_postnorm_kernel, tr=tr, n_pad=n_pad),
        grid=(m // tr,),
        in_specs=[row, row, vec, vec],
        out_specs=[row, row],
        out_shape=[jax.ShapeDtypeStruct((m, d), F32), jax.ShapeDtypeStruct((m, d), BF16)],
        compiler_params=_params(("parallel",), 2 * tr * d * (4 + 4 + 4 + 2)),
        name="postnorm",
    )(h, o, w_post.reshape(1, d), w_pre.reshape(1, d))


def _lastnorm_kernel(h_ref, o_ref, wpost_ref, y_ref):
    y_ref[...] = h_ref[...] + _rms(o_ref[...].astype(F32), wpost_ref[...])


def _lastnorm(h, o, w_post, row0, nrows):
    d = h.shape[1]
    tr = _tile(math.gcd(row0, nrows), ROW_TILE)
    rb0 = row0 // tr
    return pl.pallas_call(
        _lastnorm_kernel,
        grid=(nrows // tr,),
        in_specs=[pl.BlockSpec((tr, d), lambda i: (rb0 + i, 0)), pl.BlockSpec((tr, d), lambda i: (rb0 + i, 0)),
                  pl.BlockSpec((1, d), lambda i: (0, 0))],
        out_specs=pl.BlockSpec((tr, d), lambda i: (i, 0)),
        out_shape=jax.ShapeDtypeStruct((nrows, d), F32),
        compiler_params=_params(("parallel",), 2 * tr * d * 12),
        name="lastnorm",
    )(h, o, w_post.reshape(1, d))


def _ssd_kernel(z_ref, x_ref, b_ref, c_ref, cwx_ref, cwb_ref, cwc_ref, cbx_ref, cbb_ref, cbc_ref,
                bufx_ref, bufb_ref, bufc_ref, dtc_ref, dtr_ref, dtbc_ref, dtbr_ref, ac_ref, ar_ref,
                dsk_ref, nw_ref, h0_ref, o_ref, hout_ref, xext, bext, cext, st, ybuf,
                *, q, hg, p, n_pad, nchunks):
    c = pl.program_id(2)
    gw = hg * p

    @pl.when(c == 0)
    def _():
        xext[0:SUBLANE, :] = bufx_ref[...]
        bext[0:SUBLANE, :] = bufb_ref[...]
        cext[0:SUBLANE, :] = bufc_ref[...]
        st[...] = h0_ref[...].T

    def conv_silu(ext, raw_ref, w_ref, bias_ref):
        ext[SUBLANE:SUBLANE + q, :] = raw_ref[...]
        e = ext[...]
        z = w_ref[0:1, :] * e
        for tap in range(1, 4):
            z = pltpu.roll(z, 1, axis=0) + w_ref[tap:tap + 1, :] * e
        ext[0:SUBLANE, :] = ext[q:q + SUBLANE, :]
        return _silu(z[SUBLANE:, :] + bias_ref[...])

    xs = conv_silu(xext, x_ref, cwx_ref, cbx_ref)
    bc = conv_silu(bext, b_ref, cwb_ref, cbb_ref)
    cc = conv_silu(cext, c_ref, cwc_ref, cbc_ref)

    rows_c = c * q + lax.broadcasted_iota(jnp.int32, (q, 1), 0)
    rows_r = c * q + lax.broadcasted_iota(jnp.int32, (1, q), 1)
    dt_c = jnp.where(rows_c >= n_pad, _softplus(dtc_ref[...] + dtbc_ref[...]), 0.0)
    dt_r = jnp.where(rows_r >= n_pad, _softplus(dtr_ref[...] + dtbr_ref[...]), 0.0)
    ii = lax.broadcasted_iota(jnp.int32, (q, q), 0)
    jj = lax.broadcasted_iota(jnp.int32, (q, q), 1)
    tri = jj <= ii
    tril = jnp.where(tri, 1.0, 0.0).astype(BF16)
    triu = jnp.where(ii <= jj, 1.0, 0.0).astype(BF16)
    acum = _dot_f32_rhs(tril, dt_c * ac_ref[...])
    acum_t = _dot_f32_lhs(dt_r * ar_ref[...], triu)

    eh = lax.broadcasted_iota(jnp.int32, (LANE, gw), 0)
    el = lax.broadcasted_iota(jnp.int32, (LANE, gw), 1)
    expand = jnp.where(el // p == eh, 1.0, 0.0).astype(BF16)
    dt_x = _dot_f32_lhs(dt_c, expand)
    acum_x = _dot_f32_lhs(acum, expand)
    last_x = acum_x[q - 1:q, :]
    decay_in = jnp.exp(acum_x)
    decay_out = jnp.exp(last_x - acum_x)
    decay_all = jnp.exp(last_x)

    xdt = xs * dt_x
    xdt_b = xdt.astype(BF16)
    xw_b = (xdt * decay_out).astype(BF16)
    bc_t = bc.T.astype(BF16)
    cc_b = cc.astype(BF16)
    cb = _dot(cc_b, bc_t)
    lane_head = lax.broadcasted_iota(jnp.int32, (q, LANE), 1) // p
    heads_per_slab = LANE // p

    for s in range(gw // LANE):
        sl = slice(s * LANE, (s + 1) * LANE)
        y = _dot(cc_b, st[:, sl].astype(BF16)) * decay_in[:, sl]
        xd = xdt_b[:, sl]
        for r in range(heads_per_slab):
            hh = s * heads_per_slab + r
            seg = acum[:, hh:hh + 1] - acum_t[hh:hh + 1, :]
            dec = jnp.exp(jnp.where(tri, seg, NEG))
            mh = (cb * dec).astype(BF16)
            y = y + _dot(mh, jnp.where(lane_head == r, xd, jnp.zeros_like(xd)))
        st[:, sl] = st[:, sl] * decay_all[:, sl] + _dot(bc_t, xw_b[:, sl])
        ybuf[:, sl] = y + xs[:, sl] * dsk_ref[:, sl]

    g = ybuf[...] * _silu(z_ref[...])
    g = g * lax.rsqrt(jnp.mean(g * g, axis=-1, keepdims=True) + EPS)
    o_ref[...] = (g * nw_ref[...]).astype(o_ref.dtype)

    @pl.when(c == nchunks - 1)
    def _():
        hout_ref[...] = st[...].T


def _ssd_scan(zx, dt_c, dt_r, prm, conv_buf8, h0, *, q, nseq, nchunks, row0, n_pad, out_prev, m, name):
    g_, hg, p, n, inner = prm['G'], prm['hg'], prm['P'], prm['N'], prm['inner']
    gw = hg * p
    hgp = dt_r.shape[2]
    assert row0 % q == 0
    rb0 = row0 // q
    xo, bo, co = inner // gw, (2 * inner) // n, (2 * inner + g_ * n) // n
    cbo, cco = inner // n, (inner + g_ * n) // n

    def rmap(s, g, c):
        return rb0 + s * nchunks + c

    in_specs = [
        pl.BlockSpec((q, gw), lambda s, g, c: (rmap(s, g, c), g)),
        pl.BlockSpec((q, gw), lambda s, g, c: (rmap(s, g, c), xo + g)),
        pl.BlockSpec((q, n), lambda s, g, c: (rmap(s, g, c), bo + g)),
        pl.BlockSpec((q, n), lambda s, g, c: (rmap(s, g, c), co + g)),
        pl.BlockSpec((4, gw), lambda s, g, c: (0, g)),
        pl.BlockSpec((4, n), lambda s, g, c: (0, cbo + g)),
        pl.BlockSpec((4, n), lambda s, g, c: (0, cco + g)),
        pl.BlockSpec((1, gw), lambda s, g, c: (0, g)),
        pl.BlockSpec((1, n), lambda s, g, c: (0, cbo + g)),
        pl.BlockSpec((1, n), lambda s, g, c: (0, cco + g)),
        pl.BlockSpec((None, SUBLANE, gw), lambda s, g, c: (s, 0, g)),
        pl.BlockSpec((None, SUBLANE, n), lambda s, g, c: (s, 0, cbo + g)),
        pl.BlockSpec((None, SUBLANE, n), lambda s, g, c: (s, 0, cco + g)),
        pl.BlockSpec((None, q, LANE), lambda s, g, c: (g, rmap(s, g, c), 0)),
        pl.BlockSpec((None, None, hgp, q), lambda s, g, c: (g, s * nchunks + c, 0, 0)),
        pl.BlockSpec((None, 1, LANE), lambda s, g, c: (g, 0, 0)),
        pl.BlockSpec((None, hgp, 1), lambda s, g, c: (g, 0, 0)),
        pl.BlockSpec((None, 1, LANE), lambda s, g, c: (g, 0, 0)),
        pl.BlockSpec((None, hgp, 1), lambda s, g, c: (g, 0, 0)),
        pl.BlockSpec((1, gw), lambda s, g, c: (0, g)),
        pl.BlockSpec((1, gw), lambda s, g, c: (0, g)),
        pl.BlockSpec((None, None, gw, n), lambda s, g, c: (s, g, 0, 0)),
    ]
    args = [zx, zx, zx, zx, prm['conv_w'], prm['conv_w'], prm['conv_w'], prm['conv_b'], prm['conv_b'],
            prm['conv_b'], conv_buf8, conv_buf8, conv_buf8, dt_c, dt_r, prm['dtb_c'], prm['dtb_r'],
            prm['a_c'], prm['a_r'], prm['d_x'], prm['norm_w'], h0]
    aliases = {}
    if out_prev is not None:
        in_specs.append(pl.BlockSpec(memory_space=pl.ANY))
        args.append(out_prev)
        aliases = {len(args) - 1: 0}

    def body(*refs):
        if out_prev is not None:
            refs = refs[:22] + refs[23:]
        _ssd_kernel(*refs, q=q, hg=hg, p=p, n_pad=n_pad, nchunks=nchunks)

    vmem = (2 * (2 * q * gw * 4 + 2 * q * n * 4 + q * gw * 2 + 2 * n * gw * 4) + (2 * q + 8) * gw * 4
            + n * gw * 4 + 24 * q * gw * 4)
    return pl.pallas_call(
        body,
        grid=(nseq, g_, nchunks),
        in_specs=in_specs,
        out_specs=[pl.BlockSpec((q, gw), lambda s, g, c: (rmap(s, g, c), g)),
                   pl.BlockSpec((None, None, gw, n), lambda s, g, c: (s, g, 0, 0))],
        out_shape=[jax.ShapeDtypeStruct((m, inner), BF16),
                   jax.ShapeDtypeStruct((nseq, g_, gw, n), F32)],
        scratch_shapes=[pltpu.VMEM((q + SUBLANE, gw), F32), pltpu.VMEM((q + SUBLANE, n), F32),
                        pltpu.VMEM((q + SUBLANE, n), F32), pltpu.VMEM((n, gw), F32),
                        pltpu.VMEM((q, gw), F32)],
        input_output_aliases=aliases,
        compiler_params=_params(("parallel", "parallel", "arbitrary"), vmem),
        name=name,
    )(*args)


def _ssd_layer(u, lay, j, w_in, conv_w, conv_b, dt_bias, a_log, d_skip, norm_w, w_out, conv_state, ssd_state):
    m = u.shape[0]
    lp, nb, ls, n_pad = lay['lp'], lay['nb'], lay['ls'], lay['n_pad']
    h_, = dt_bias.shape
    inner = w_out.shape[1]
    p = inner // h_
    n = ssd_state.shape[-1]
    conv_dim = conv_w.shape[1]
    g_ = (conv_dim - inner) // (2 * n)
    hg = h_ // g_
    gw = hg * p
    assert LANE % p == 0 and gw % LANE == 0 and n % LANE == 0 and hg <= LANE
    hgp = -(-hg // SUBLANE) * SUBLANE

    zx = _mm(u, w_in, layer=j, n_cols=2 * inner + 2 * g_ * n, out_dtype=F32, name="ssd_in")
    dt_raw = _mm(u, w_in, layer=j, n_cols=h_, col_off=inner + conv_dim, out_dtype=F32, tn=min(h_, LANE),
                 name="ssd_dt")
    dt_c = jnp.pad(dt_raw.reshape(m, g_, hg).transpose(1, 0, 2), ((0, 0), (0, 0), (0, LANE - hg)))

    def dt_rows(q, start, stop):
        return dt_c[:, start:stop, :hgp].reshape(g_, (stop - start) // q, q, hgp).transpose(0, 1, 3, 2)

    a = -jnp.exp(a_log.astype(F32)).reshape(g_, hg)
    dtb = dt_bias.astype(F32).reshape(g_, hg)
    prm = dict(
        G=g_, hg=hg, P=p, N=n, inner=inner,
        conv_w=conv_w, conv_b=conv_b.reshape(1, conv_dim),
        dtb_c=jnp.pad(dtb, ((0, 0), (0, LANE - hg)))[:, None, :],
        dtb_r=jnp.pad(dtb, ((0, 0), (0, hgp - hg)))[:, :, None],
        a_c=jnp.pad(a, ((0, 0), (0, LANE - hg)))[:, None, :],
        a_r=jnp.pad(a, ((0, 0), (0, hgp - hg)))[:, :, None],
        d_x=jnp.repeat(d_skip.astype(F32), p).reshape(1, inner),
        norm_w=norm_w.reshape(1, inner),
    )

    def buf8(buf):
        return jnp.pad(buf.astype(F32), ((0, 0), (SUBLANE - 3, 0), (0, 0)))

    qp = SSD_CHUNK
    assert lp % qp == 0
    y, hp = _ssd_scan(zx, dt_c, dt_rows(qp, 0, lp), prm, jnp.zeros((1, SUBLANE, conv_dim), F32),
                      jnp.zeros((1, g_, gw, n), F32), q=qp, nseq=1, nchunks=lp // qp, row0=0,
                      n_pad=n_pad, out_prev=None, m=m, name="ssd_scan_prompt")
    y, hs = _ssd_scan(zx, dt_c, dt_rows(ls, lp, m), prm, buf8(conv_state),
                      ssd_state.astype(F32).reshape(nb, g_, gw, n), q=ls, nseq=nb,
                      nchunks=1, row0=lp, n_pad=0, out_prev=y, m=m, name="ssd_scan_sample")
    o = _mm(y, w_out, layer=j, n_cols=w_out.shape[2], out_dtype=F32, name="ssd_out")

    xbc = zx[:, inner:]
    conv_p = xbc[lp - 3:lp][None]
    conv_s = xbc[lp:].reshape(nb, ls, conv_dim)[:, ls - 3:]
    return o, hp.reshape(1, h_, p, n), hs.reshape(nb, h_, p, n), conv_p, conv_s


def _cmul(pr, pi, xr, xi):
    return pr * xr - pi * xi, pr * xi + pi * xr


def _s5_kernel(v_ref, um_ref, km_ref, cm_ref, p_ref, d_ref, h0_ref, y_ref, hout_ref, carry, yscr,
               um_s, km_s, cm_s, *, rb, seg, nseg, sw, gl, s_, p_):
    t_ = S5_T

    @pl.when(pl.program_id(1) == 0)
    def _():
        carry[...] = h0_ref[...]

        def expand(c_ref, dst, row_div, col_div, inner):
            cw = c_ref.shape[1]
            n_out = dst.shape[1]
            r = lax.broadcasted_iota(jnp.int32, (cw, n_out), 0)
            c = lax.broadcasted_iota(jnp.int32, (cw, n_out), 1)
            rep = jnp.where((r // inner == c // (gl * inner)) & (r % inner == c % inner), 1.0, 0.0).astype(BF16)
            full = _dot(c_ref[...], rep)
            rr = lax.broadcasted_iota(jnp.int32, full.shape, 0)
            cc = lax.broadcasted_iota(jnp.int32, full.shape, 1)
            same = (rr // row_div) % gl == (cc // col_div) % gl
            dst[...] = jnp.where(same, full, 0.0).astype(BF16)

        expand(um_ref, um_s, s_, p_, p_)
        expand(km_ref, km_s, s_, s_, s_)
        expand(cm_ref, cm_s, p_, s_, s_)

    vs = [v_ref[pl.ds(i, rb, stride=t_), :] for i in range(t_)]
    vcat = jnp.concatenate(vs, axis=1).astype(BF16)
    x = _dot(vcat, um_s[...])
    rowk = lax.broadcasted_iota(jnp.int32, (rb, 1), 0) % seg
    hin = carry[...]
    if nseg == 1:
        hin_rows = jnp.broadcast_to(hin, (rb, 2 * sw))
    else:
        hin_rows = jnp.broadcast_to(hin[:, None, :], (nseg, seg, 2 * sw)).reshape(rb, 2 * sw)

    row8 = lax.broadcasted_iota(jnp.int32, (rb, 1), 0) % SUBLANE
    s = 1
    while s < SUBLANE:
        sh = jnp.where(row8 >= s, pltpu.roll(x, s, axis=0), 0.0)
        ar, ai = _cmul(p_ref[s:s + 1, :sw], p_ref[s:s + 1, sw:], sh[:, :sw], sh[:, sw:])
        x = x + jnp.concatenate([ar, ai], axis=1)
        s *= 2
    p8r, p8i = p_ref[1:SUBLANE + 1, :sw], p_ref[1:SUBLANE + 1, sw:]
    if seg == SUBLANE:
        ar, ai = _cmul(jnp.concatenate([p8r] * nseg, axis=0), jnp.concatenate([p8i] * nseg, axis=0),
                       hin_rows[:, :sw], hin_rows[:, sw:])
        x = x + jnp.concatenate([ar, ai], axis=1)
    else:
        groups = []
        prev = hin_rows[0:SUBLANE, :]
        for gi in range(seg // SUBLANE):
            ar, ai = _cmul(p8r, p8i, prev[:, :sw], prev[:, sw:])
            groups.append(x[gi * SUBLANE:(gi + 1) * SUBLANE, :] + jnp.concatenate([ar, ai], axis=1))
            prev = jnp.broadcast_to(groups[-1][SUBLANE - 1:SUBLANE, :], (SUBLANE, 2 * sw))
        x = jnp.concatenate(groups, axis=0)
    hb = jnp.where(rowk == 0, hin_rows, pltpu.roll(x, 1, axis=0))

    y_all = _dot(vcat, km_s[...]) + _dot(hb.astype(BF16), cm_s[...])
    for j in range(t_):
        yj = y_all[:, j * LANE:(j + 1) * LANE] + vs[j] * d_ref[...]
        yscr[pl.ds(j, rb, stride=t_), :] = _gelu_tanh(yj)
    y_ref[...] = yscr[...].astype(y_ref.dtype)

    if nseg == 1:
        xlast = x[seg - 1:seg, :]
    else:
        pick = (lax.broadcasted_iota(jnp.int32, (nseg, rb), 1)
                == lax.broadcasted_iota(jnp.int32, (nseg, rb), 0) * seg + (seg - 1))
        xlast = _dot_f32_rhs(jnp.where(pick, 1.0, 0.0).astype(BF16), x)
    carry[...] = xlast
    hout_ref[...] = xlast


def _s5_scan(vg, mats, ptab, d_row, h0, *, rb, seg, nseg, nblocks, m, name):
    um, km, cm = mats
    nt, tl, p2 = um.shape
    p_ = p2 // 2
    s_ = km.shape[2] // S5_T
    gl = LANE // s_
    sw = gl * p_
    sw2 = 2 * sw
    width = nt * LANE
    rows = rb * S5_T
    assert seg % SUBLANE == 0 and (nseg == 1 or seg == SUBLANE) and rb == nseg * seg
    in_specs = [
        pl.BlockSpec((rows, LANE), lambda t, r: (r, t)),
        pl.BlockSpec((None, tl, p2), lambda t, r: (t, 0, 0)),
        pl.BlockSpec((None, tl, km.shape[2]), lambda t, r: (t, 0, 0)),
        pl.BlockSpec((None, sw2, cm.shape[2]), lambda t, r: (t, 0, 0)),
        pl.BlockSpec((None, ptab.shape[1], sw2), lambda t, r: (t, 0, 0)),
        pl.BlockSpec((1, LANE), lambda t, r: (0, t)),
        pl.BlockSpec((None, nseg, sw2), lambda t, r: (t, 0, 0)),
    ]
    vmem = 2 * rows * LANE * 6 + (tl * sw2 + tl * tl + sw2 * tl) * 2 + 3 * tl * sw2 * 4 + 12 * rb * sw2 * 4
    return pl.pallas_call(
        functools.partial(_s5_kernel, rb=rb, seg=seg, nseg=nseg, sw=sw, gl=gl, s_=s_, p_=p_),
        grid=(nt, nblocks),
        in_specs=in_specs,
        out_specs=[pl.BlockSpec((rows, LANE), lambda t, r: (r, t)),
                   pl.BlockSpec((None, nseg, sw2), lambda t, r: (t, 0, 0))],
        out_shape=[jax.ShapeDtypeStruct((m, width), BF16),
                   jax.ShapeDtypeStruct((nt, nseg, sw2), F32)],
        scratch_shapes=[pltpu.VMEM((nseg, sw2), F32), pltpu.VMEM((rows, LANE), F32),
                        pltpu.VMEM((tl, sw2), BF16), pltpu.VMEM((tl, tl), BF16), pltpu.VMEM((sw2, tl), BF16)],
        compiler_params=_params(("parallel", "arbitrary"), vmem),
        name=name,
    )(vg, um, km, cm, ptab, d_row, h0)


def _s5_tables(lam_re, lam_im, log_step, b_re, b_im, c_re, c_im):
    t_ = S5_T
    g_, p = lam_re.shape
    s_ = b_re.shape[-1]
    gl = LANE // s_
    nt = g_ // gl
    lam = lax.complex(lam_re.astype(F32), lam_im.astype(F32))
    delta = jnp.exp(log_step.astype(F32))[:, None]
    lam_bar = jnp.exp(lam * delta)
    b_bar = ((lam_bar - 1.0) / lam)[..., None] * lax.complex(b_re.astype(F32), b_im.astype(F32))
    c_t = lax.complex(c_re.astype(F32), c_im.astype(F32))

    def power(d):
        d = jnp.asarray(d, F32)[..., None, None]
        return jnp.exp((lam * delta) * d)

    pw = power(jnp.arange(t_ + 1))
    ub = pw[t_ - 1 - jnp.arange(t_)][:, :, :, None] * b_bar[None]
    ub = ub.transpose(1, 0, 3, 2).reshape(nt, gl, t_, s_, p).transpose(0, 2, 1, 3, 4)
    um = jnp.stack([ub.real, ub.imag], axis=4).astype(BF16).reshape(nt, t_ * gl * s_, 2 * p)

    kd = jnp.einsum('gop,dgp,gpi->dgio', c_t, pw[:t_], b_bar).real
    ii = jnp.arange(t_)[:, None]
    jj = jnp.arange(t_)[None, :]
    kf = jnp.where((jj >= ii)[:, :, None, None, None], kd[jnp.clip(jj - ii, 0, t_ - 1)], 0.0)
    kf = kf.transpose(2, 0, 3, 1, 4).reshape(nt, gl, t_, s_, t_, s_).transpose(0, 2, 1, 3, 4, 5)
    km = kf.astype(BF16).reshape(nt, t_ * gl * s_, t_ * s_)

    cw = c_t[None] * pw[1:][:, :, None, :]
    cw = cw.transpose(1, 3, 0, 2).reshape(nt, gl, p, t_, s_)
    cm = jnp.stack([cw.real, -cw.imag], axis=1).astype(BF16).reshape(nt, 2 * gl * p, t_ * s_)

    nrow = 2 * SUBLANE
    pt = power(t_ * jnp.arange(nrow))
    pt = pt.reshape(nrow, nt, gl * p).transpose(1, 0, 2)
    ptab = jnp.concatenate([pt.real, pt.imag], axis=-1)
    return (um, km, cm), ptab


def _s5_layer(u, lay, w_in, lam_re, lam_im, log_step, b_re, b_im, c_re, c_im, d_skip, w_glu, b_glu, w_out,
              state):
    m = u.shape[0]
    lp, nb, ls = lay['lp'], lay['nb'], lay['ls']
    width = w_glu.shape[0]
    g_, p = lam_re.shape
    s_ = b_re.shape[-1]
    gl = LANE // s_
    nt = g_ // gl
    sw = gl * p
    assert g_ * s_ == width and LANE % s_ == 0 and ls % S5_T == 0 and lp % S5_T == 0

    vg = _mm(u, w_in, n_cols=2 * width, out_dtype=F32, name="s5_in")
    rb_p = _tile(lp // S5_T, S5_ROWS)
    seg_s = ls // S5_T
    mats, ptab = _s5_tables(lam_re, lam_im, log_step, b_re, b_im, c_re, c_im)
    d_row = d_skip.astype(F32).reshape(1, width)

    def to_tiles(st):
        b = st.shape[0]
        return st.astype(F32).reshape(b, nt, gl, p, 2).transpose(1, 0, 4, 2, 3).reshape(nt, b, 2 * sw)

    def from_tiles(h):
        b = h.shape[1]
        return h.reshape(nt, b, 2, gl, p).transpose(1, 0, 3, 4, 2).reshape(b, g_, p, 2)

    y, hp = _s5_scan(vg, mats, ptab, d_row, jnp.zeros((nt, 1, 2 * sw), F32), rb=rb_p, seg=rb_p, nseg=1,
                     nblocks=lp // (rb_p * S5_T), m=m, name="s5_scan_prompt")
    y_s, hs = _s5_scan(vg[lp:], mats, ptab, d_row, to_tiles(state), rb=nb * seg_s, seg=seg_s, nseg=nb,
                       nblocks=1, m=nb * ls, name="s5_scan_sample")
    y = lax.dynamic_update_slice(y, y_s, (lp, 0))

    def glu(acc, yv, gate, bias):
        return yv.astype(F32) * _sigmoid(acc + bias) * _silu(gate)

    t = _mm(y, w_glu, n_cols=width, out_dtype=BF16, epilogue=glu,
            extras=((y, 'tile', 0), (vg, 'tile', width), (b_glu.reshape(1, width), 'row', 0)), name="s5_glu")
    o = _mm(t, w_out, n_cols=w_out.shape[1], out_dtype=F32, name="s5_out")
    return o, from_tiles(hp), from_tiles(hs)


def _rope_tile(t, cos_ref, sina_ref, sinb_ref, half):
    return (t * cos_ref[...] + pltpu.roll(t, LANE - half, axis=1) * sina_ref[...]
            + pltpu.roll(t, half, axis=1) * sinb_ref[...])


def _mla_prep_kernel(c_ref, qw_ref, kvw_ref, cos_ref, sina_ref, sinb_ref,
                     cqn_ref, ckv_ref, kr_ref, ckr_ref, *, qr, kvr, rope):
    c = c_ref[...]
    cqn_ref[...] = _rms(c[:, :qr], qw_ref[...]).astype(cqn_ref.dtype)
    ckv = _rms(c[:, qr:qr + kvr], kvw_ref[...])
    ckv_ref[...] = ckv
    rot = _rope_tile(c[:, qr + kvr:qr + kvr + LANE], cos_ref, sina_ref, sinb_ref, rope // 2)
    kr_ref[...] = rot[:, :rope]
    ckr_ref[...] = jnp.concatenate([ckv, rot], axis=1).astype(ckr_ref.dtype)


def _mla_prep(gc, col_off, wc, q_norm, kv_norm, tabs, *, qr, kvr, rope):
    m = gc.shape[0]
    tr = _tile(m, ROW_TILE)
    assert col_off % wc == 0 and wc >= qr + kvr + LANE
    cb = col_off // wc
    row = lambda w: pl.BlockSpec((tr, w), lambda i: (i, 0))
    return pl.pallas_call(
        functools.partial(_mla_prep_kernel, qr=qr, kvr=kvr, rope=rope),
        grid=(m // tr,),
        in_specs=[pl.BlockSpec((tr, wc), lambda i: (i, cb)),
                  pl.BlockSpec((1, qr), lambda i: (0, 0)), pl.BlockSpec((1, kvr), lambda i: (0, 0)),
                  row(LANE), row(LANE), row(LANE)],
        out_specs=[row(qr), row(kvr), row(rope), row(kvr + LANE)],
        out_shape=[jax.ShapeDtypeStruct((m, qr), BF16), jax.ShapeDtypeStruct((m, kvr), F32),
                   jax.ShapeDtypeStruct((m, rope), F32), jax.ShapeDtypeStruct((m, kvr + LANE), BF16)],
        compiler_params=_params(("parallel",), 2 * tr * (wc * 4 + 3 * LANE * 4 + qr * 2 + kvr * 4 + (kvr + LANE) * 2)),
        name="mla_prep",
    )(gc, q_norm.reshape(1, qr), kv_norm.reshape(1, kvr), *tabs)


def _attn_tile(qs, k, v, m_i, l_i, acc, mask):
    s = lax.dot_general(qs, k, (((1,), (1,)), ((), ())), preferred_element_type=F32)
    if mask is not None:
        s = jnp.where(mask, s, NEG)
    m_new = jnp.maximum(m_i, jnp.max(s, axis=1, keepdims=True))
    alpha = jnp.exp(m_i - m_new)
    pexp = jnp.exp(s - m_new)
    l_new = alpha * l_i + jnp.sum(pexp, axis=1, keepdims=True)
    acc_new = alpha * acc + _dot(pexp.astype(BF16), v)
    return m_new, l_new, acc_new


def _q_scaled(q_ref, cos_ref, sina_ref, sinb_ref, *, nope, rope, scale):
    qf = q_ref[...].astype(F32)
    qrot = _rope_tile(qf[:, nope:], cos_ref, sina_ref, sinb_ref, rope // 2)
    return (jnp.concatenate([qf[:, :nope], qrot], axis=1) * scale).astype(BF16)


def _attn_prompt_kernel(q_ref, cos_ref, sina_ref, sinb_ref, k_ref, v_ref, g_ref, o_ref,
                        q_scr, s_buf, p_buf, a_buf, m_scr, l_scr, acc_scr, bias_buf,
                        *, tq, tk, nope, rope, dv, scale, n_pad, first):
    i = pl.program_id(1)
    q_scr[...] = _q_scaled(q_ref, cos_ref, sina_ref, sinb_ref, nope=nope, rope=rope,
                           scale=scale * math.log2(math.e))

    def chunk_of(r):
        return jnp.where(r < first, 0, 1 + (r - first) // CHUNK)

    qchunk = chunk_of(i * tq + lax.broadcasted_iota(jnp.int32, (tq, 1), 0))
    col = lax.broadcasted_iota(jnp.int32, (1, tk), 1)
    ja = (i * tq) // tk
    jl = (i * tq + tq - 1) // tk

    def bias_of(j):
        kcol = j * tk + col
        vis = jnp.where(kcol >= n_pad, chunk_of(kcol), jnp.iinfo(jnp.int32).max) <= qchunk
        return jnp.where(vis, 0.0, 2.0 * NEG)

    @pl.when(i == 0)
    def _():
        bias_buf[0] = jnp.zeros((tq, tk), F32)
        bias_buf[4] = jnp.full((tq, tk), 2.0 * NEG, F32)
        s_buf[1] = jnp.zeros((tq, tk), F32)

    @pl.when(i == -(-tk // tq))
    def _():
        bias_buf[1] = bias_of(0)

    bias_buf[2] = bias_of(ja)
    bias_buf[3] = bias_of(jl)
    p_buf[0] = jnp.zeros((tq, tk), BF16)
    a_buf[0] = jnp.ones((tq, LANE), F32)
    m_scr[...] = jnp.full((tq, LANE), NEG, F32)
    l_scr[...] = jnp.zeros((tq, LANE), F32)
    acc_scr[...] = jnp.zeros((tq, dv), F32)
    nl = tk // LANE

    def stage(t, sa, sb):
        ks = pl.multiple_of(jnp.minimum(t, jl) * tk, tk)
        s_buf[sa] = lax.dot_general(q_scr[...], k_ref[pl.ds(ks, tk), :], (((1,), (1,)), ((), ())),
                                    preferred_element_type=F32)
        vs = pl.multiple_of(jnp.clip(t - 2, 0, jl) * tk, tk)
        acc_scr[...] = a_buf[sa] * acc_scr[...] + _dot(p_buf[sa], v_ref[pl.ds(vs, tk), :])
        j = t - 1
        kind = jnp.where((j < 0) | (j > jl), 4,
                         jnp.where(j == jl, 3, jnp.where(j == ja, 2, jnp.where(j == 0, 1, 0))))
        s = s_buf[sb] + bias_buf[kind]
        m_old = m_scr[...]
        m_new = jnp.maximum(m_old, jnp.max(s, axis=1, keepdims=True))
        a_new = jnp.exp2(m_old - m_new)
        pf = jnp.exp2(s - jnp.concatenate([m_new] * nl, axis=1))
        part = pf[:, 0:LANE]
        for c in range(1, nl):
            part = part + pf[:, c * LANE:(c + 1) * LANE]
        l_scr[...] = a_new * l_scr[...] + part
        m_scr[...] = m_new
        a_buf[sb] = a_new
        p_buf[sb] = pf.astype(BF16)

    def stages(t0, count):
        for u in range(0, count, 2):
            stage(t0 + u, 0, 1)
            stage(t0 + u + 1, 1, 0)

    def eight_stages(tt, carry):
        stages(8 * tt, 8)
        return carry

    n_stage = jl + 3
    n8 = n_stage // 8
    lax.fori_loop(0, n8, eight_stages, 0)
    rest = n_stage - 8 * n8
    has4 = rest >= 4
    base = 8 * n8 + jnp.where(has4, 4, 0)
    rest2 = rest - jnp.where(has4, 4, 0)

    @pl.when(has4)
    def _():
        stages(8 * n8, 4)

    @pl.when(rest2 >= 1)
    def _():
        stages(base, 2)

    @pl.when(rest2 == 3)
    def _():
        stages(base + 2, 2)
    l_i = jnp.sum(l_scr[...], axis=1, keepdims=True)
    o_ref[...] = (acc_scr[...] / l_i * _silu(g_ref[...].astype(F32))).astype(o_ref.dtype)


def _attn_sample_kernel(q_ref, cos_ref, sina_ref, sinb_ref, k_ref, v_ref, g_ref, prev_ref, o_ref,
                        *, nope, rope, dv, scale, n_valid):
    del prev_ref
    qs = _q_scaled(q_ref, cos_ref, sina_ref, sinb_ref, nope=nope, rope=rope, scale=scale)
    lq, lk = qs.shape[0], k_ref.shape[0]
    mask = lax.broadcasted_iota(jnp.int32, (1, lk), 1) < n_valid
    carry = (jnp.full((lq, 1), NEG, F32), jnp.zeros((lq, 1), F32), jnp.zeros((lq, dv), F32))
    _, l_i, acc = _attn_tile(qs, k_ref[...], v_ref[...], *carry, mask)
    o_ref[...] = (acc / l_i * _silu(g_ref[...].astype(F32))).astype(o_ref.dtype)


def _mla_layer(u, lay, w_in, q_norm, w_uq, kv_norm, w_ukv, w_out, cache_ckv, cache_kr, cache_meta_ckv,
               cache_meta_kr):
    m, d = u.shape
    lp, nb, ls, n_pad, first, n_meta = lay['lp'], lay['nb'], lay['ls'], lay['n_pad'], lay['first'], lay['n_meta']
    qr, = q_norm.shape
    kvr, = kv_norm.shape
    rope = cache_kr.shape[-1]
    width = w_out.shape[0]
    heads = (w_uq.shape[1] - w_ukv.shape[1] + width) // rope
    nope = w_uq.shape[1] // heads - rope
    dv = width // heads
    past = cache_ckv.shape[1]
    assert nope == LANE and dv == LANE and 2 * rope == LANE and (qr + kvr) % LANE == 0 and n_meta > 0
    dk = nope + LANE
    scale = float(nope + rope) ** -0.5

    wc = min(c for c in range(qr + kvr + LANE, width + 1, LANE) if width % c == 0)
    w_perm = jnp.concatenate([w_in[:, qr + kvr + rope:], w_in[:, :qr + kvr + rope],
                              jnp.zeros((d, wc - (qr + kvr + rope)), w_in.dtype)], axis=1)
    ncol = width + wc
    gc = _mm(u, w_perm, n_cols=ncol, out_dtype=F32, name="mla_in")

    pos = np.concatenate([np.maximum(np.arange(lp) - n_pad, 0),
                          np.tile(n_meta + past + np.arange(ls), nb)]).astype(np.float32)
    half = rope // 2
    inv = ROPE_THETA ** (-jnp.arange(half, dtype=F32) / half)
    ang = jnp.asarray(pos)[:, None] * inv[None, :]
    cos, sin, zero = jnp.cos(ang), jnp.sin(ang), jnp.zeros((m, LANE - rope), F32)
    zh = jnp.zeros((m, half), F32)
    tabs = (jnp.concatenate([cos, cos, zero], axis=1), jnp.concatenate([-sin, zh, zero], axis=1),
            jnp.concatenate([zh, sin, zero], axis=1))

    cqn, ckv, kr, ckr = _mla_prep(gc, width, wc, q_norm, kv_norm, tabs, qr=qr, kvr=kvr, rope=rope)

    wq = jnp.pad(w_uq.reshape(qr, heads, nope + rope), ((0, 0), (0, 0), (0, LANE - rope))).reshape(qr, heads * dk)
    wkv = w_ukv.reshape(kvr, heads, nope + dv)
    eye = jnp.pad(jnp.eye(rope, dtype=w_ukv.dtype), ((0, LANE - rope), (0, LANE - rope)))
    wk = jnp.concatenate([
        jnp.pad(wkv[:, :, :nope], ((0, 0), (0, 0), (0, LANE))),
        jnp.pad(jnp.broadcast_to(eye[:, None, :], (LANE, heads, LANE)), ((0, 0), (0, 0), (nope, 0)))],
        axis=0).reshape(kvr + LANE, heads * dk)
    wv = jnp.pad(wkv[:, :, nope:], ((0, LANE), (0, 0), (0, 0))).reshape(kvr + LANE, heads * dv)

    tq = _tile(lp, ATTN_Q_ROWS, LANE)
    tk = ATTN_K_ROWS
    lkv = -(-lp // tk) * tk
    if not (tq <= tk and lkv <= m):
        tk, lkv = tq, lp
    q = _mm(cqn, wq, n_cols=heads * dk, out_dtype=BF16, name="mla_q")
    k_p = _mm(ckr, wk, n_cols=heads * dk, out_dtype=BF16, rows=lkv, name="mla_k_prompt")
    v_p = _mm(ckr, wv, n_cols=heads * dv, out_dtype=BF16, rows=lkv, name="mla_v_prompt")

    s_chunk = 1 + (past + np.arange(ls)) // CHUNK
    k_chunk = np.concatenate([1 + np.arange(past) // CHUNK, s_chunk, np.zeros(n_meta, np.int64)])
    assert (k_chunk[None, :] <= s_chunk[:, None]).all()
    n_valid = past + ls + n_meta
    lk = -(-n_valid // LANE) * LANE

    def cat_rows(ckv_part, kr_part):
        return jnp.concatenate([ckv_part.astype(BF16), kr_part.astype(BF16),
                                jnp.zeros(ckv_part.shape[:-1] + (LANE - rope,), BF16)], axis=-1)

    ckr_s = jnp.concatenate([
        cat_rows(cache_ckv, cache_kr), ckr[lp:].reshape(nb, ls, kvr + LANE),
        cat_rows(cache_meta_ckv, cache_meta_kr), jnp.zeros((nb, lk - n_valid, kvr + LANE), BF16)],
        axis=1).reshape(nb * lk, kvr + LANE)
    k_s = _mm(ckr_s, wk, n_cols=heads * dk, out_dtype=BF16, name="mla_k_sample")
    v_s = _mm(ckr_s, wv, n_cols=heads * dv, out_dtype=BF16, name="mla_v_sample")

    assert first % CHUNK == 0 and tq % CHUNK == 0 and first <= tq <= tk and lkv >= 2 * tk
    kw = dict(nope=nope, rope=rope, dv=dv, scale=scale)
    qspec = lambda rows, f: [pl.BlockSpec((rows, dk), f), pl.BlockSpec((rows, LANE), lambda *a: (f(*a)[0], 0)),
                             pl.BlockSpec((rows, LANE), lambda *a: (f(*a)[0], 0)),
                             pl.BlockSpec((rows, LANE), lambda *a: (f(*a)[0], 0))]
    o = pl.pallas_call(
        functools.partial(_attn_prompt_kernel, tq=tq, tk=tk, n_pad=n_pad, first=first, **kw),
        grid=(heads, lp // tq),
        in_specs=qspec(tq, lambda h, i: (i, h)) + [
            pl.BlockSpec((lkv, dk), lambda h, i: (0, h)), pl.BlockSpec((lkv, dv), lambda h, i: (0, h)),
            pl.BlockSpec((tq, dv), lambda h, i: (i, h))],
        out_specs=pl.BlockSpec((tq, dv), lambda h, i: (i, h)),
        out_shape=jax.ShapeDtypeStruct((m, width), BF16),
        scratch_shapes=[pltpu.VMEM((tq, dk), BF16), pltpu.VMEM((2, tq, tk), F32), pltpu.VMEM((2, tq, tk), BF16),
                        pltpu.VMEM((2, tq, LANE), F32), pltpu.VMEM((tq, LANE), F32), pltpu.VMEM((tq, LANE), F32),
                        pltpu.VMEM((tq, dv), F32), pltpu.VMEM((5, tq, tk), F32)],
        compiler_params=_params(("parallel", "arbitrary"), 2 * lkv * (dk + dv) * 2 + 20 * tq * tk * 4),
        name="mla_attn_prompt",
    )(q, *tabs, k_p, v_p, gc)
    assert lp % ls == 0
    rb0 = lp // ls
    o = pl.pallas_call(
        functools.partial(_attn_sample_kernel, n_valid=n_valid, **kw),
        grid=(nb, heads),
        in_specs=qspec(ls, lambda b, h: (rb0 + b, h)) + [
            pl.BlockSpec((lk, dk), lambda b, h: (b, h)), pl.BlockSpec((lk, dv), lambda b, h: (b, h)),
            pl.BlockSpec((ls, dv), lambda b, h: (rb0 + b, h)), pl.BlockSpec(memory_space=pl.ANY)],
        out_specs=pl.BlockSpec((ls, dv), lambda b, h: (rb0 + b, h)),
        out_shape=jax.ShapeDtypeStruct((m, width), BF16),
        input_output_aliases={7: 0},
        compiler_params=_params(("parallel", "arbitrary"), 2 * lk * (dk + dv) * 2 + 8 * ls * lk * 4),
        name="mla_attn_sample",
    )(q, *tabs, k_s, v_s, gc, o)
    out = _mm(o, w_out, n_cols=w_out.shape[1], out_dtype=F32, name="mla_out")
    return out, ckv, kr


def kernel(x_prompt, x_sample, state_ssd, state_conv, state_s5, cache_ckv, cache_kr, cache_meta_ckv, cache_meta_kr, meta_tokens, norm_pre, norm_post, ssd_w_in, ssd_conv_w, ssd_conv_b, ssd_dt_bias, ssd_a_log, ssd_d, ssd_norm, ssd_w_out, s5_w_in, s5_lam_re, s5_lam_im, s5_log_step, s5_b_re, s5_b_im, s5_c_re, s5_c_im, s5_d, s5_w_glu, s5_b_glu, s5_w_out, mla_w_in, mla_q_norm, mla_w_uq, mla_kv_norm, mla_w_ukv, mla_w_out):
    bp, seq, d = x_prompt.shape
    nb, ls, _ = x_sample.shape
    n_meta = meta_tokens.shape[0]
    depth = norm_pre.shape[0]
    assert bp == 1 and seq % LANE == 0 and ls == CHUNK
    n_pad = (-n_meta) % LANE
    first = n_pad + n_meta
    lp = first + seq
    m = lp + nb * ls
    lay = dict(lp=lp, nb=nb, ls=ls, n_pad=n_pad, first=first, n_meta=n_meta)

    h = jnp.concatenate([jnp.zeros((n_pad, d), F32), meta_tokens.astype(F32), x_prompt[0].astype(F32),
                         x_sample.reshape(nb * ls, d).astype(F32)], axis=0)
    u = _prenorm(h, norm_pre[0])

    ssd_p, ssd_s, conv_p, conv_s, s5_p, s5_s = [], [], [], [], [], []
    ckv_all, kr_all = [], []
    for i in range(depth):
        kind, j = i % 3, i // 3
        if kind == 0:
            o, sp, ss, cp, cs = _ssd_layer(u, lay, j, ssd_w_in, ssd_conv_w[j], ssd_conv_b[j], ssd_dt_bias[j],
                                           ssd_a_log[j], ssd_d[j], ssd_norm[j], ssd_w_out, state_conv[j],
                                           state_ssd[j])
            ssd_p.append(sp.astype(state_ssd.dtype))
            ssd_s.append(ss.astype(state_ssd.dtype))
            conv_p.append(cp)
            conv_s.append(cs)
        elif kind == 1:
            o, hpp, hss = _s5_layer(u, lay, s5_w_in[j], s5_lam_re[j], s5_lam_im[j], s5_log_step[j], s5_b_re[j],
                                    s5_b_im[j], s5_c_re[j], s5_c_im[j], s5_d[j], s5_w_glu[j], s5_b_glu[j],
                                    s5_w_out[j], state_s5[j])
            s5_p.append(hpp.astype(state_s5.dtype))
            s5_s.append(hss.astype(state_s5.dtype))
        else:
            o, ckv, kr = _mla_layer(u, lay, mla_w_in[j], mla_q_norm[j], mla_w_uq[j], mla_kv_norm[j],
                                    mla_w_ukv[j], mla_w_out[j], cache_ckv[j], cache_kr[j], cache_meta_ckv[j],
                                    cache_meta_kr[j])
            ckv_all.append(ckv)
            kr_all.append(kr)
        if i + 1 < depth:
            h, u = _postnorm(h, o, norm_post[i], norm_pre[i + 1], n_pad)
        else:
            y_prompt = _lastnorm(h, o, norm_post[i], first, lp - first)[None]
            y_sample = _lastnorm(h, o, norm_post[i], lp, nb * ls).reshape(nb, ls, d)

    def split(a):
        return (a[:, None, n_pad:first], a[:, None, first:lp], a[:, lp:].reshape(a.shape[0], nb, ls, a.shape[-1]))

    mckv_p, ckv_p, ckv_s = split(jnp.stack(ckv_all))
    mkr_p, kr_p, kr_s = split(jnp.stack(kr_all))
    return (y_prompt, y_sample, jnp.stack(ssd_p), jnp.stack(ssd_s), jnp.stack(conv_p), jnp.stack(conv_s),
            jnp.stack(s5_p), jnp.stack(s5_s), mckv_p, mkr_p, ckv_p, kr_p, ckv_s, kr_s)
```
